```python
import math
import jax
import jax.numpy as jnp
from jax import lax
import numpy as np

D_MODEL = 2048
BATCH = 4
SEQ = 4096
DEPTH = 2
DEC_BATCH = 128
DEC_SEQ = 1
PAST_LEN = 16384
PAGE_SIZE = 128

RWKV_HEADS = 16
RWKV_HEAD_DIM = 64
RWKV_DIM = RWKV_HEADS * RWKV_HEAD_DIM
W_LORA = 64
A_LORA = 64
V_LORA = 32
G_LORA = 128
RWKV_IN = 3 * RWKV_DIM + W_LORA + A_LORA + G_LORA
RWKV_GN_EPS = 64e-5
SSM_HEADS = 16
SSM_HEAD_DIM = 64
SSM_DIM = SSM_HEADS * SSM_HEAD_DIM
SSM_STATE = 128
SSM_GROUPS = 2
SSM_HPG = SSM_HEADS // SSM_GROUPS
CONV_W = 4
CONV_DIM = SSM_DIM + 2 * SSM_GROUPS * SSM_STATE
SSM_IN = SSM_DIM + CONV_DIM + SSM_HEADS
SSD_CHUNK = 128
MLA_HEADS = 16
NOPE_DIM = 64
ROPE_DIM = 32
QK_DIM = NOPE_DIM + ROPE_DIM
V_HEAD_DIM = 64
MLA_DIM = MLA_HEADS * V_HEAD_DIM
Q_LORA = 512
KV_LORA = 256
MLA_IN = Q_LORA + KV_LORA + ROPE_DIM
ROPE_THETA = 10000.0
Q_BLOCK = 128
ATTN_SCALE = QK_DIM ** -0.5
N_BRANCH = 3
BRANCH_DIM = 1024
GATE_IN = N_BRANCH * D_MODEL
IN_WIDTH = RWKV_IN + SSM_IN + MLA_IN + GATE_IN
N_EXPERTS = 64
TOP_K = 8
N_EXPERT_GROUPS = 8
TOPK_GROUPS = 4
EXPERT_FF = 512
SHARED_FF = 512
ROUTED_SCALE = 2.5
NORM_EPS = 1e-6
NEG_INF = -1e30

kernel_name = 'hybrid_rwkv7_ssd_mla_moe_step'

F32 = jnp.float32


def _rms_norm(x, g, eps=NORM_EPS):
    xf = x.astype(F32)
    y = xf * lax.rsqrt(jnp.mean(xf * xf, axis=-1, keepdims=True) + eps)
    return (y * g.astype(F32)).astype(x.dtype)


def _rope_tables(pos):
    inv = 1.0 / (ROPE_THETA ** (jnp.arange(0, ROPE_DIM, 2, dtype=F32) / ROPE_DIM))
    ang = pos.astype(F32)[:, None] * inv[None, :]
    ang = jnp.concatenate([ang, ang], axis=-1)
    return jnp.cos(ang), jnp.sin(ang)


def _apply_rope(x, cos, sin):
    half = ROPE_DIM // 2
    rot = jnp.concatenate([-x[..., half:], x[..., :half]], axis=-1)
    return (x.astype(F32) * cos + rot.astype(F32) * sin).astype(x.dtype)


def _qk_gain(g):
    return jnp.concatenate([g, g[NOPE_DIM:]], axis=-1)


def _rwkv_recurrence(r, decay, k, v, kk, a, s0):
    def step(s, inp):
        r_t, w_t, k_t, v_t, kk_t, a_t = inp
        sa = jnp.einsum('bhvk,bhk->bhv', s, -kk_t)
        s = (s * w_t[:, :, None, :] + sa[..., None] * (kk_t * a_t)[:, :, None, :]
             + v_t[..., None] * k_t[:, :, None, :])
        return s, jnp.einsum('bhvk,bhk->bhv', s, r_t)
    xs = tuple(jnp.swapaxes(t, 0, 1) for t in (r, decay, k, v, kk, a))
    s_fin, ys = lax.scan(step, s0, xs)
    return jnp.swapaxes(ys, 0, 1), s_fin


def _rwkv_branch(zA, shift0, wkv0, l, W, v_first):
    B, L, _ = zA.shape
    H, N = RWKV_HEADS, RWKV_HEAD_DIM
    prev = jnp.concatenate([shift0[:, None, :].astype(zA.dtype), zA[:, :-1]], axis=1)
    zs = zA + W['rwkv_mu'][l] * (prev - zA)
    cuts = [RWKV_DIM, 2 * RWKV_DIM, 3 * RWKV_DIM, 3 * RWKV_DIM + W_LORA, 3 * RWKV_DIM + W_LORA + A_LORA]
    r, k, v, xw, xa, xg = jnp.split(zs, cuts, axis=-1)
    vecs = W['rwkv_vecs'][l].astype(F32)
    w0, a0, k_k, k_a, r_k, ln_w, ln_b = (vecs[i] for i in range(7))
    logw = -jax.nn.softplus(-(w0 + (jnp.tanh(xw) @ W['rwkv_w_up'][l]).astype(F32))) - 0.5
    decay = jnp.exp(-jnp.exp(logw))
    a = jax.nn.sigmoid(a0 + (xa @ W['rwkv_a_up'][l]).astype(F32))
    g = jax.nn.sigmoid(xg) @ W['rwkv_g_up'][l]
    if l == 0:
        v_first = v
    else:
        gate_v = jax.nn.sigmoid(W['rwkv_v0'][l - 1] + (v @ W['rwkv_v_down'][l - 1]) @ W['rwkv_v_up'][l - 1])
        v = v + (v_first - v) * gate_v
    heads = lambda t: t.reshape(B, L, H, N).astype(F32)
    kf = k.astype(F32)
    kk = heads(kf * k_k)
    kk = kk / jnp.maximum(jnp.sqrt(jnp.sum(kk * kk, axis=-1, keepdims=True)), 1e-12)
    k_mod = heads(kf * (1.0 + (a - 1.0) * k_a))
    rf, vf = heads(r), heads(v)
    y, wkv_new = _rwkv_recurrence(rf, heads(decay), k_mod, vf, kk, heads(a), wkv0.astype(F32))
    mean = jnp.mean(y, axis=-1, keepdims=True)
    var = jnp.mean(jnp.square(y - mean), axis=-1, keepdims=True)
    y = ((y - mean) * lax.rsqrt(var + RWKV_GN_EPS)).reshape(B, L, RWKV_DIM) * ln_w + ln_b
    bonus = jnp.sum(rf * k_mod * r_k.reshape(H, N), axis=-1, keepdims=True) * vf
    y = (y + bonus.reshape(B, L, RWKV_DIM)) * g.astype(F32)
    return y.astype(zA.dtype), zA[:, -1], wkv_new, v_first


def _segsum(x):
    T = x.shape[-1]
    xr = jnp.broadcast_to(x[..., :, None], x.shape + (T,))
    idx = jnp.arange(T)
    ss = jnp.cumsum(jnp.where(idx[:, None] > idx[None, :], xr, 0.0), axis=-2)
    return jnp.where(idx[:, None] >= idx[None, :], ss, -jnp.inf)


def _ssd_scan(X, A, Bg, Cg, s0):
    b, L, H, P = X.shape
    Q = SSD_CHUNK if L % SSD_CHUNK == 0 else L
    nc = L // Q
    Bh = jnp.repeat(Bg, SSM_HPG, axis=2).reshape(b, nc, Q, H, SSM_STATE)
    Ch = jnp.repeat(Cg, SSM_HPG, axis=2).reshape(b, nc, Q, H, SSM_STATE)
    Xc = X.reshape(b, nc, Q, H, P)
    Ac = jnp.transpose(A.reshape(b, nc, Q, H), (0, 3, 1, 2))
    A_cs = jnp.cumsum(Ac, axis=-1)
    Lmat = jnp.exp(_segsum(Ac))
    scores = jnp.einsum('bclhn,bcshn->bhcls', Ch, Bh) * Lmat
    y_diag = jnp.einsum('bhcls,bcshp->bclhp', scores, Xc)
    decay_states = jnp.exp(A_cs[..., -1:] - A_cs)
    states = jnp.einsum('bclhn,bhcl,bclhp->bchpn', Bh, decay_states, Xc)
    states = jnp.concatenate([s0[:, None], states], axis=1)
    chunk_tot = jnp.pad(A_cs[..., -1], ((0, 0), (0, 0), (1, 0)))
    decay_chunk = jnp.exp(_segsum(chunk_tot))
    new_states = jnp.einsum('bhzc,bchpn->bzhpn', decay_chunk, states)
    states, s_fin = new_states[:, :-1], new_states[:, -1]
    y_off = jnp.einsum('bclhn,bchpn,bhcl->bclhp', Ch, states, jnp.exp(A_cs))
    return (y_diag + y_off).reshape(b, L, H, P), s_fin


def _ssd_branch(zB, conv0, ssm0, l, W):
    B, L, _ = zB.shape
    zg, xbc, dt = jnp.split(zB, [SSM_DIM, SSM_DIM + CONV_DIM], axis=-1)
    xpad = jnp.concatenate([conv0.astype(xbc.dtype), xbc], axis=1)
    cw = W['ssm_conv_w'][l]
    conv = W['ssm_conv_b'][l] + xpad[:, 0:L] * cw[0]
    for i in range(1, CONV_W):
        conv = conv + xpad[:, i:i + L] * cw[i]
    xc = jax.nn.silu(conv)
    xs, bm, cm = jnp.split(xc, [SSM_DIM, SSM_DIM + SSM_GROUPS * SSM_STATE], axis=-1)
    xs_h = xs.reshape(B, L, SSM_HEADS, SSM_HEAD_DIM).astype(F32)
    dtp = jax.nn.softplus(dt.astype(F32) + W['ssm_dt_bias'][l].astype(F32))
    A = -jnp.exp(W['ssm_a_log'][l].astype(F32))
    y, s_new = _ssd_scan(xs_h * dtp[..., None], dtp * A,
                         bm.reshape(B, L, SSM_GROUPS, SSM_STATE).astype(F32),
                         cm.reshape(B, L, SSM_GROUPS, SSM_STATE).astype(F32), ssm0.astype(F32))
    y = y + W['ssm_d'][l].astype(F32)[:, None] * xs_h
    y = y.reshape(B, L, SSM_DIM) * jax.nn.silu(zg.astype(F32))
    gs = SSM_DIM // SSM_GROUPS
    y = _rms_norm(y.reshape(B, L, SSM_GROUPS, gs), W['ssm_norm_w'][l].reshape(SSM_GROUPS, gs)).reshape(B, L, SSM_DIM)
    return y.astype(zB.dtype), xpad[:, -(CONV_W - 1):], s_new


def _mla_kv(lat, kpe, kv_up, gain_k):
    kv = jnp.einsum('bnc,chd->bnhd', lat, kv_up)
    k_nope, v = kv[..., :NOPE_DIM], kv[..., NOPE_DIM:]
    k_rope = jnp.broadcast_to(kpe[:, :, None, :].astype(k_nope.dtype), k_nope.shape[:-1] + (ROPE_DIM,))
    k = _rms_norm(jnp.concatenate([k_nope, k_rope], axis=-1), _qk_gain(gain_k))
    return k, v


def _attend_causal(q, k, v):
    B, S, H, _ = q.shape
    qb_len = Q_BLOCK if S % Q_BLOCK == 0 else S
    nb = S // qb_len
    qb = jnp.swapaxes(q.reshape(B, nb, qb_len, H, QK_DIM), 0, 1)
    kpos = jnp.arange(S)
    def block(args):
        qi, i = args
        s = jnp.einsum('bqhd,bkhd->bhqk', qi, k).astype(F32) * ATTN_SCALE
        qpos = i * qb_len + jnp.arange(qb_len)
        s = jnp.where(kpos[None, :] <= qpos[:, None], s, NEG_INF)
        p = jax.nn.softmax(s, axis=-1)
        return jnp.einsum('bhqk,bkhd->bqhd', p.astype(v.dtype), v)
    o = lax.map(block, (qb, jnp.arange(nb)))
    return jnp.swapaxes(o, 0, 1).reshape(B, S, H, V_HEAD_DIM)


def _online_merge(carry, s, v):
    m, den, acc = carry
    m_new = jnp.maximum(m, jnp.max(s, axis=-1))
    corr = jnp.exp(m - m_new)
    p = jnp.exp(s - m_new[..., None])
    den = den * corr + jnp.sum(p, axis=-1)
    acc = acc * corr[..., None] + jnp.einsum('bhtk,bkhd->bhtd', p, v.astype(F32))
    return (m_new, den, acc)


def _mla_branch(zC, pos, l, W, attend):
    B, L, _ = zC.shape
    qd, kvd, kpe = jnp.split(zC, [Q_LORA, Q_LORA + KV_LORA], axis=-1)
    cos, sin = _rope_tables(pos)
    q = jnp.einsum('blr,rhd->blhd', _rms_norm(qd, W['mla_q_norm'][l]), W['mla_q_up'][l])
    q = jnp.concatenate([q[..., :NOPE_DIM],
                         _apply_rope(q[..., NOPE_DIM:], cos[None, :, None], sin[None, :, None])], axis=-1)
    q = _rms_norm(q, _qk_gain(W['mla_qk_gain_q'][l]))
    lat = _rms_norm(kvd, W['mla_kv_norm'][l])
    kpe = _apply_rope(kpe, cos[None], sin[None])
    k, v = _mla_kv(lat, kpe, W['mla_kv_up'][l], W['mla_qk_gain_k'][l])
    o = attend(q, k, v, l)
    return o.reshape(B, L, MLA_DIM), lat, kpe


def _moe(h, l, W):
    B, L, D = h.shape
    t = h.reshape(B * L, D)
    T = B * L
    per = N_EXPERTS // N_EXPERT_GROUPS
    s = jax.nn.sigmoid((t @ W['w_router'][l]).astype(F32))
    sb = s + W['router_bias'][l].astype(F32)
    gscore = jnp.sum(lax.top_k(sb.reshape(T, N_EXPERT_GROUPS, per), 2)[0], axis=-1)
    _, gidx = lax.top_k(gscore, TOPK_GROUPS)
    gmask = jnp.sum(jax.nn.one_hot(gidx, N_EXPERT_GROUPS, dtype=F32), axis=1)
    emask = jnp.repeat(gmask, per, axis=1) > 0
    _, eidx = lax.top_k(jnp.where(emask, sb, NEG_INF), TOP_K)
    wsel = jnp.take_along_axis(s, eidx, axis=1)
    wsel = wsel / jnp.sum(wsel, axis=-1, keepdims=True) * ROUTED_SCALE
    gate = jnp.sum(jax.nn.one_hot(eidx, N_EXPERTS, dtype=F32) * wsel[..., None], axis=1).astype(h.dtype)
    def expert(acc, e):
        wg, wu, wd, ge = e
        hid = jax.nn.silu(t @ wg) * (t @ wu)
        return acc + ge[:, None] * (hid @ wd), None
    routed, _ = lax.scan(expert, jnp.zeros_like(t),
                         (W['w_exp_gate'][l], W['w_exp_up'][l], W['w_exp_down'][l], gate.T))
    shared = (jax.nn.silu(t @ W['w_sh_gate'][l]) * (t @ W['w_sh_up'][l])) @ W['w_sh_down'][l]
    return (routed + shared).reshape(B, L, D)


def _layer(x, c, pos, l, shift0, wkv0, conv0, ssm0, attend, W, v_first):
    B, L, _ = x.shape
    mod = jax.nn.silu(c) @ W['w_ada'][l] + W['b_ada'][l]
    sh1, sc1, g1, sh2, sc2, g2 = jnp.split(mod[:, None, :], 6, axis=-1)
    h = _rms_norm(x, W['norm_mix'][l]) * (1.0 + sc1) + sh1
    z = h @ W['w_in'][l]
    zA, zB, zC, zG = jnp.split(z, [RWKV_IN, RWKV_IN + SSM_IN, RWKV_IN + SSM_IN + MLA_IN], axis=-1)
    ya, shift_new, wkv_new, v_first = _rwkv_branch(zA, shift0, wkv0, l, W, v_first)
    yb, conv_new, ssm_new = _ssd_branch(zB, conv0, ssm0, l, W)
    yc, lat, kpe = _mla_branch(zC, pos, l, W, attend)
    gates = jax.nn.sigmoid(zG.astype(F32)).astype(x.dtype).reshape(B, L, N_BRANCH, D_MODEL)
    wb = W['w_branch'][l]
    merged = gates[:, :, 0] * (ya @ wb[0])
    merged = merged + gates[:, :, 1] * (yb @ wb[1])
    merged = merged + gates[:, :, 2] * (yc @ wb[2])
    x = x + g1 * (merged @ W['w_out'][l])
    h2 = _rms_norm(x, W['norm_ffn'][l]) * (1.0 + sc2) + sh2
    x = x + g2 * _moe(h2, l, W)
    st = (lat, kpe, shift_new, wkv_new.astype(x.dtype), conv_new, ssm_new.astype(x.dtype))
    return x, st, v_first


def _trunk(x, c, pos, shift_s, wkv_s, conv_s, ssm_s, attend, W):
    outs = ([], [], [], [], [], [])
    v_first = None
    for l in range(DEPTH):
        x, st, v_first = _layer(x, c, pos, l, shift_s[l], wkv_s[l], conv_s[l], ssm_s[l], attend, W, v_first)
        for lst, arr in zip(outs, st):
            lst.append(arr)
    return (x,) + tuple(jnp.stack(o) for o in outs)


def setup_inputs(seed: int = 0) -> dict:
    key = jax.random.key(seed)
    keys = jax.random.split(key, 96)
    cnt = [0]
    def nk():
        cnt[0] += 1
        return keys[cnt[0] - 1]
    def nrm(shape, scale):
        return jax.random.normal(nk(), shape, F32) * scale
    def gain(shape):
        return 1.0 + nrm(shape, 0.02)
    n_pages = PAST_LEN // PAGE_SIZE
    n_phys = (DEC_BATCH * n_pages * 5) // 4
    Dm = D_MODEL
    x_prompt = nrm((BATCH, SEQ, Dm), 1.0)
    x_sample = nrm((DEC_BATCH, DEC_SEQ, Dm), 1.0)
    cache_mla_latent = nrm((DEPTH, n_phys, PAGE_SIZE, KV_LORA), 1.0)
    cache_mla_rope = nrm((DEPTH, n_phys, PAGE_SIZE, ROPE_DIM), 1.0)
    perm = jax.random.permutation(nk(), n_phys)
    page_table = perm[:DEC_BATCH * n_pages].reshape(DEC_BATCH, n_pages).astype(jnp.int32)
    state_rwkv_shift = nrm((DEPTH, DEC_BATCH, RWKV_IN), 1.0)
    state_rwkv_wkv = nrm((DEPTH, DEC_BATCH, RWKV_HEADS, RWKV_HEAD_DIM, RWKV_HEAD_DIM), 0.5)
    state_ssm_conv = nrm((DEPTH, DEC_BATCH, CONV_W - 1, CONV_DIM), 1.0)
    state_ssm = nrm((DEPTH, DEC_BATCH, SSM_HEADS, SSM_HEAD_DIM, SSM_STATE), 0.5)
    c_prompt = nrm((BATCH, Dm), 1.0)
    c_sample = nrm((DEC_BATCH, Dm), 1.0)
    rv = (DEPTH, RWKV_DIM)
    rwkv_vecs = jnp.stack([
        jax.random.uniform(nk(), rv, F32, -6.0, -1.0),
        nrm(rv, 0.1),
        0.85 + nrm(rv, 0.02),
        1.0 + nrm(rv, 0.02),
        nrm(rv, 0.1),
        gain(rv),
        nrm(rv, 0.02),
    ], axis=1)
    dtv = jnp.exp(jax.random.uniform(nk(), (DEPTH, SSM_HEADS), F32, math.log(1e-3), math.log(0.1)))
    return {
        'x_prompt': x_prompt,
        'x_sample': x_sample,
        'cache_mla_latent': cache_mla_latent,
        'cache_mla_rope': cache_mla_rope,
        'page_table': page_table,
        'state_rwkv_shift': state_rwkv_shift,
        'state_rwkv_wkv': state_rwkv_wkv,
        'state_ssm_conv': state_ssm_conv,
        'state_ssm': state_ssm,
        'c_prompt': c_prompt,
        'c_sample': c_sample,
        'norm_mix': gain((DEPTH, Dm)),
        'norm_ffn': gain((DEPTH, Dm)),
        'w_ada': nrm((DEPTH, Dm, 6 * Dm), 0.5 * Dm ** -0.5),
        'b_ada': nrm((DEPTH, 6 * Dm), 0.02),
        'w_in': nrm((DEPTH, Dm, IN_WIDTH), Dm ** -0.5),
        'rwkv_mu': jax.random.uniform(nk(), (DEPTH, RWKV_IN), F32, 0.0, 1.0),
        'rwkv_vecs': rwkv_vecs,
        'rwkv_w_up': nrm((DEPTH, W_LORA, RWKV_DIM), 0.1 * W_LORA ** -0.5),
        'rwkv_a_up': nrm((DEPTH, A_LORA, RWKV_DIM), 0.5 * A_LORA ** -0.5),
        'rwkv_g_up': nrm((DEPTH, G_LORA, RWKV_DIM), G_LORA ** -0.5),
        'rwkv_v0': nrm((DEPTH - 1, RWKV_DIM), 0.1),
        'rwkv_v_down': nrm((DEPTH - 1, RWKV_DIM, V_LORA), RWKV_DIM ** -0.5),
        'rwkv_v_up': nrm((DEPTH - 1, V_LORA, RWKV_DIM), 0.5 * V_LORA ** -0.5),
        'ssm_conv_w': nrm((DEPTH, CONV_W, CONV_DIM), CONV_W ** -0.5),
        'ssm_conv_b': nrm((DEPTH, CONV_DIM), 0.02),
        'ssm_dt_bias': dtv + jnp.log(-jnp.expm1(-dtv)),
        'ssm_a_log': jnp.log(jax.random.uniform(nk(), (DEPTH, SSM_HEADS), F32, 1.0, 16.0)),
        'ssm_d': 1.0 + nrm((DEPTH, SSM_HEADS), 0.1),
        'ssm_norm_w': gain((DEPTH, SSM_DIM)),
        'mla_q_norm': gain((DEPTH, Q_LORA)),
        'mla_q_up': nrm((DEPTH, Q_LORA, MLA_HEADS, QK_DIM), Q_LORA ** -0.5),
        'mla_kv_norm': gain((DEPTH, KV_LORA)),
        'mla_kv_up': nrm((DEPTH, KV_LORA, MLA_HEADS, NOPE_DIM + V_HEAD_DIM), KV_LORA ** -0.5),
        'mla_qk_gain_q': gain((DEPTH, NOPE_DIM + ROPE_DIM // 2)),
        'mla_qk_gain_k': gain((DEPTH, NOPE_DIM + ROPE_DIM // 2)),
        'w_branch': nrm((DEPTH, N_BRANCH, BRANCH_DIM, Dm), BRANCH_DIM ** -0.5),
        'w_out': nrm((DEPTH, Dm, Dm), Dm ** -0.5),
        'w_router': nrm((DEPTH, Dm, N_EXPERTS), Dm ** -0.5),
        'router_bias': nrm((DEPTH, N_EXPERTS), 0.01),
        'w_exp_gate': nrm((DEPTH, N_EXPERTS, Dm, EXPERT_FF), Dm ** -0.5),
        'w_exp_up': nrm((DEPTH, N_EXPERTS, Dm, EXPERT_FF), Dm ** -0.5),
        'w_exp_down': nrm((DEPTH, N_EXPERTS, EXPERT_FF, Dm), EXPERT_FF ** -0.5),
        'w_sh_gate': nrm((DEPTH, Dm, SHARED_FF), Dm ** -0.5),
        'w_sh_up': nrm((DEPTH, Dm, SHARED_FF), Dm ** -0.5),
        'w_sh_down': nrm((DEPTH, SHARED_FF, Dm), SHARED_FF ** -0.5),
    }


def reference(x_prompt, x_sample, cache_mla_latent, cache_mla_rope, page_table,
              state_rwkv_shift, state_rwkv_wkv, state_ssm_conv, state_ssm, c_prompt, c_sample,
              norm_mix, norm_ffn, w_ada, b_ada, w_in, rwkv_mu, rwkv_vecs, rwkv_w_up, rwkv_a_up,
              rwkv_g_up, rwkv_v0, rwkv_v_down, rwkv_v_up, ssm_conv_w, ssm_conv_b, ssm_dt_bias,
              ssm_a_log, ssm_d, ssm_norm_w, mla_q_norm, mla_q_up, mla_kv_norm, mla_kv_up,
              mla_qk_gain_q, mla_qk_gain_k, w_branch, w_out, w_router, router_bias, w_exp_gate,
              w_exp_up, w_exp_down, w_sh_gate, w_sh_up, w_sh_down):
    W = dict(norm_mix=norm_mix, norm_ffn=norm_ffn, w_ada=w_ada, b_ada=b_ada, w_in=w_in,
             rwkv_mu=rwkv_mu, rwkv_vecs=rwkv_vecs, rwkv_w_up=rwkv_w_up, rwkv_a_up=rwkv_a_up,
             rwkv_g_up=rwkv_g_up, rwkv_v0=rwkv_v0, rwkv_v_down=rwkv_v_down, rwkv_v_up=rwkv_v_up,
             ssm_conv_w=ssm_conv_w, ssm_conv_b=ssm_conv_b, ssm_dt_bias=ssm_dt_bias,
             ssm_a_log=ssm_a_log, ssm_d=ssm_d, ssm_norm_w=ssm_norm_w, mla_q_norm=mla_q_norm,
             mla_q_up=mla_q_up, mla_kv_norm=mla_kv_norm, mla_kv_up=mla_kv_up,
             mla_qk_gain_q=mla_qk_gain_q, mla_qk_gain_k=mla_qk_gain_k, w_branch=w_branch,
             w_out=w_out, w_router=w_router, router_bias=router_bias, w_exp_gate=w_exp_gate,
             w_exp_up=w_exp_up, w_exp_down=w_exp_down, w_sh_gate=w_sh_gate, w_sh_up=w_sh_up,
             w_sh_down=w_sh_down)

    Bp, Sp = x_prompt.shape[0], x_prompt.shape[1]
    dt_ = x_prompt.dtype
    pos_p = jnp.arange(Sp, dtype=jnp.int32)
    attend_prompt = lambda q, k, v, l: _attend_causal(q, k, v)
    (y_prompt, p_lat, p_rope, p_shift, p_wkv, p_conv, p_ssm) = _trunk(
        x_prompt, c_prompt, pos_p,
        jnp.zeros((DEPTH, Bp, RWKV_IN), dt_),
        jnp.zeros((DEPTH, Bp, RWKV_HEADS, RWKV_HEAD_DIM, RWKV_HEAD_DIM), dt_),
        jnp.zeros((DEPTH, Bp, CONV_W - 1, CONV_DIM), dt_),
        jnp.zeros((DEPTH, Bp, SSM_HEADS, SSM_HEAD_DIM, SSM_STATE), dt_),
        attend_prompt, W)

    Ts = x_sample.shape[1]
    pos_s = PAST_LEN + jnp.arange(Ts, dtype=jnp.int32)

    def attend_paged(q, k_new, v_new, l):
        lat_pool, kpe_pool = cache_mla_latent[l], cache_mla_rope[l]
        kv_up, gk = mla_kv_up[l], mla_qk_gain_k[l]
        Bd, T = q.shape[0], q.shape[1]
        def page_step(carry, pidx):
            k, v = _mla_kv(lat_pool[pidx], kpe_pool[pidx], kv_up, gk)
            s = jnp.einsum('bthd,bkhd->bhtk', q, k).astype(F32) * ATTN_SCALE
            return _online_merge(carry, s, v), None
        init = (jnp.full((Bd, MLA_HEADS, T), NEG_INF, F32),
                jnp.zeros((Bd, MLA_HEADS, T), F32),
                jnp.zeros((Bd, MLA_HEADS, T, V_HEAD_DIM), F32))
        carry, _ = lax.scan(page_step, init, page_table.T)
        s = jnp.einsum('bthd,bkhd->bhtk', q, k_new).astype(F32) * ATTN_SCALE
        tpos = jnp.arange(T)
        s = jnp.where(tpos[None, :] <= tpos[:, None], s, NEG_INF)
        _, den, acc = _online_merge(carry, s, v_new)
        return jnp.swapaxes(acc / den[..., None], 1, 2).astype(q.dtype)

    (y_sample, s_lat, s_rope, s_shift, s_wkv, s_conv, s_ssm) = _trunk(
        x_sample, c_sample, pos_s, state_rwkv_shift, state_rwkv_wkv, state_ssm_conv, state_ssm,
        attend_paged, W)

    return (y_prompt, y_sample, p_lat, p_rope, p_shift, p_wkv, p_conv, p_ssm,
            s_lat, s_rope, s_shift, s_wkv, s_conv, s_ssm)
```

```python
import functools
import math

import jax
import jax.numpy as jnp
import numpy as np
from jax import lax
from jax.experimental import pallas as pl
from jax.experimental.pallas import tpu as pltpu

F32 = jnp.float32
BF16 = jnp.bfloat16

LANES = 128
SUBLANES = 8
MXU_DIM = 256
VMEM_LIMIT = 56 * 1024 * 1024

D_MODEL = 2048
HEADS = 16
HD = 64
RWKV_DIM = HEADS * HD
W_LORA, A_LORA, V_LORA, G_LORA = 64, 64, 32, 128
RWKV_IN = 3 * RWKV_DIM + W_LORA + A_LORA + G_LORA
RWKV_GN_EPS = 64e-5
SSM_DIM = HEADS * HD
SSM_STATE = 128
SSM_GROUPS = 2
CONV_W = 4
CONV_DIM = SSM_DIM + 2 * SSM_GROUPS * SSM_STATE
SSM_IN = SSM_DIM + CONV_DIM + HEADS
SSD_CHUNK = 128
NOPE_DIM, ROPE_DIM = 64, 32
QK_DIM = NOPE_DIM + ROPE_DIM
Q_LORA, KV_LORA = 512, 256
MLA_IN = Q_LORA + KV_LORA + ROPE_DIM
ROPE_THETA = 10000.0
ATTN_SCALE = QK_DIM ** -0.5
N_BRANCH = 3
N_EXPERTS = 64
TOP_K = 8
N_EXPERT_GROUPS = 8
TOPK_GROUPS = 4
EXPERT_FF = 512
ROUTED_SCALE = 2.5
NORM_EPS = 1e-6
NEG_INF = -1e30
PAGE = 128

TM_PROMPT = 512
TQ_FLASH = 512
TM_MOE = 128
TM_EXPERT = 256
SCAN_TB = 32


def _cparams(sem):
    return pltpu.CompilerParams(dimension_semantics=sem, vmem_limit_bytes=VMEM_LIMIT)


def _dot(a, b):
    return jnp.dot(a, b, preferred_element_type=F32)


def _dot_nt(a, b):
    return lax.dot_general(a, b, (((1,), (1,)), ((), ())), preferred_element_type=F32)


def _split2(x):
    hi = x.astype(BF16)
    lo = (x - hi.astype(F32)).astype(BF16)
    return hi, lo


def _split3(x):
    hi = x.astype(BF16)
    r = x - hi.astype(F32)
    mid = r.astype(BF16)
    lo = (r - mid.astype(F32)).astype(BF16)
    return hi, mid, lo


def _dot01(x, m01):
    a, b, c = _split3(x)
    return _dot(a, m01) + _dot(b, m01) + _dot(c, m01)


def _segsum(x, bd):
    hi, lo = _split2(x)
    outs = []
    for c in range(x.shape[1] // MXU_DIM):
        sl = slice(c * MXU_DIM, (c + 1) * MXU_DIM)
        outs.append(_dot(hi[:, sl], bd) + _dot(lo[:, sl], bd))
    return outs[0] if len(outs) == 1 else jnp.concatenate(outs, axis=1)


def _softplus(x):
    return jnp.maximum(x, 0.0) + jnp.log1p(jnp.exp(-jnp.abs(x)))


def _sigmoid(x):
    return 1.0 / (1.0 + jnp.exp(-x))


def _silu(x):
    return x * _sigmoid(x)


def _block_diag01(n, seg):
    i = np.arange(n)
    return jnp.asarray((i[:, None] // seg) == (i[None, :] // seg), dtype=BF16)


class Group:
    def __init__(self, batch, seqlen, tm):
        self.B, self.L, self.T, self.tm = batch, seqlen, batch * seqlen, tm
        self.per_token = seqlen == 1
        assert self.T % tm == 0
        assert self.per_token or seqlen % tm == 0
        self.ntiles = self.T // tm

    def vec(self, a):
        return a[None] if self.per_token else a[:, None, :]

    def vec_spec(self, tn, order):
        tm, L = self.tm, self.L
        if self.per_token:
            if order == "ji":
                return pl.BlockSpec((None, tm, tn), lambda j, i: (0, i, j))
            return pl.BlockSpec((None, tm, tn), lambda i: (0, i, 0))
        if order == "ji":
            return pl.BlockSpec((None, 1, tn), lambda j, i: ((i * tm) // L, 0, j))
        return pl.BlockSpec((None, 1, tn), lambda i: ((i * tm) // L, 0, 0))


def _mm_body(*refs, act_in, epi, n_extra):
    a_ref, w_ref = refs[0], refs[1]
    extra = refs[2:2 + n_extra]
    o_ref = refs[2 + n_extra]
    wb_ref = refs[3 + n_extra]

    @pl.when(pl.program_id(1) == 0)
    def _():
        wb_ref[...] = w_ref[...].astype(BF16)

    a = a_ref[...]
    if act_in == "silu":
        a = _silu(a.astype(F32))
    acc = _dot(a.astype(BF16), wb_ref[...])
    if epi == "bias":
        acc = acc + extra[0][...]
    elif epi == "sigmoid":
        acc = _sigmoid(acc)
    elif epi == "residual":
        acc = extra[0][...] + extra[1][...] * acc
    elif epi == "residual2":
        acc = extra[0][...] + extra[1][...] * (acc + extra[2][...].astype(F32))
    o_ref[...] = acc.astype(o_ref.dtype)


def mm(a, w, tm, tn, *, act_in=None, epi=None, extra=(), extra_specs=(), out_dtype=F32):
    M, K = a.shape
    N = w.shape[1]
    assert M % tm == 0 and N % tn == 0, (M, tm, N, tn)
    body = functools.partial(_mm_body, act_in=act_in, epi=epi, n_extra=len(extra))
    return pl.pallas_call(
        body,
        grid=(N // tn, M // tm),
        in_specs=[pl.BlockSpec((tm, K), lambda j, i: (i, 0)),
                  pl.BlockSpec((K, tn), lambda j, i: (0, j))] + list(extra_specs),
        out_specs=pl.BlockSpec((tm, tn), lambda j, i: (i, j)),
        out_shape=jax.ShapeDtypeStruct((M, N), out_dtype),
        scratch_shapes=[pltpu.VMEM((K, tn), BF16)],
        compiler_params=_cparams(("arbitrary", "arbitrary")),
    )(a, w, *extra)


def _norm_mod_body(x_ref, g_ref, sc_ref, sh_ref, o_ref):
    x = x_ref[...]
    y = x * lax.rsqrt(jnp.mean(x * x, axis=-1, keepdims=True) + NORM_EPS) * g_ref[...]
    o_ref[...] = (y * (1.0 + sc_ref[...]) + sh_ref[...]).astype(o_ref.dtype)


def norm_mod(grp, x, g, sc, sh):
    T, D = x.shape
    tm = grp.tm
    return pl.pallas_call(
        _norm_mod_body,
        grid=(T // tm,),
        in_specs=[pl.BlockSpec((tm, D), lambda i: (i, 0)),
                  pl.BlockSpec((1, D), lambda i: (0, 0)),
                  grp.vec_spec(D, "i"), grp.vec_spec(D, "i")],
        out_specs=pl.BlockSpec((tm, D), lambda i: (i, 0)),
        out_shape=jax.ShapeDtypeStruct((T, D), BF16),
        compiler_params=_cparams(("arbitrary",)),
    )(x, g.reshape(1, D), sc, sh)


def _rwkv_prep_body(*refs, layer1, per_token, tiles_per_seq):
    z_ref, prev_ref, mu_ref, vec_ref, wup_ref, aup_ref, gup_ref, bd_ref = refs[:8]
    rest = refs[8:]
    if layer1:
        v0_ref, vdn_ref, vup_ref, vfirst_ref = rest[:4]
        rest = rest[4:]
    r_o, w_o, k_o, v_o, kk_o, b_o, g_o, carry = rest
    z = z_ref[...]
    tm = z.shape[0]
    if per_token:
        prev = prev_ref[...]
    else:
        first = (pl.program_id(0) % tiles_per_seq) == 0
        prow = jnp.where(first, prev_ref[...], carry[...])
        rid = lax.broadcasted_iota(jnp.int32, z.shape, 0)
        prev = jnp.where(rid == 0, prow, pltpu.roll(z, 1, 0))
        carry[...] = z[tm - 1:tm, :]
    zs = z + mu_ref[...] * (prev - z)
    D = RWKV_DIM
    r, k, v = zs[:, 0:D], zs[:, D:2 * D], zs[:, 2 * D:3 * D]
    lora = zs[:, 3 * D:3 * D + LANES]
    xg = zs[:, 3 * D + LANES:3 * D + 2 * LANES]
    w0, a0 = vec_ref[0:1, :], vec_ref[1:2, :]
    k_k, k_a = vec_ref[2:3, :], vec_ref[3:4, :]
    logw = -_softplus(-(w0 + _dot(jnp.tanh(lora).astype(BF16), wup_ref[...]))) - 0.5
    decay = jnp.exp(-jnp.exp(logw))
    a = _sigmoid(a0 + _dot(lora.astype(BF16), aup_ref[...]))
    g = _dot(_sigmoid(xg).astype(BF16), gup_ref[...])
    if layer1:
        lo = _dot(v.astype(BF16), vdn_ref[...])
        gate_v = _sigmoid(v0_ref[...] + _dot(lo.astype(BF16), vup_ref[...]))
        v = v + (vfirst_ref[...] - v) * gate_v
    kk = k * k_k
    ss = _segsum(kk * kk, bd_ref[...])
    kk = kk / jnp.maximum(jnp.sqrt(ss), 1e-12)
    r_o[...] = r
    w_o[...] = decay
    k_o[...] = k * (1.0 + (a - 1.0) * k_a)
    v_o[...] = v
    kk_o[...] = kk
    b_o[...] = kk * a
    g_o[...] = g


def rwkv_prep(grp, zA, prev, mu, vecs8, wup, aup, gup, bd, layer1_args):
    T = zA.shape[0]
    tm = grp.tm
    D = RWKV_DIM
    layer1 = layer1_args is not None
    full = lambda shp: pl.BlockSpec(shp, lambda i: tuple(0 for _ in shp))
    tile = lambda n: pl.BlockSpec((tm, n), lambda i: (i, 0))
    if grp.per_token:
        prev_spec = tile(RWKV_IN)
        tiles_per_seq = 1
    else:
        tiles_per_seq = grp.L // tm
        prev_spec = pl.BlockSpec((None, 1, RWKV_IN), lambda i: (i // tiles_per_seq, 0, 0))
    in_specs = [tile(RWKV_IN), prev_spec, full((1, RWKV_IN)), full((8, D)),
                full((LANES, D)), full((LANES, D)), full((LANES, D)), full((MXU_DIM, MXU_DIM))]
    args = [zA, prev, mu, vecs8, wup, aup, gup, bd]
    if layer1:
        v0, vdn, vup, vfirst = layer1_args
        in_specs += [full((1, D)), full((D, LANES)), full((LANES, D)), tile(D)]
        args += [v0, vdn, vup, vfirst]
    body = functools.partial(_rwkv_prep_body, layer1=layer1, per_token=grp.per_token,
                             tiles_per_seq=tiles_per_seq)
    return pl.pallas_call(
        body,
        grid=(T // tm,),
        in_specs=in_specs,
        out_specs=[tile(D)] * 7,
        out_shape=[jax.ShapeDtypeStruct((T, D), F32)] * 7,
        scratch_shapes=[pltpu.VMEM((1, RWKV_IN), F32)],
        compiler_params=_cparams(("arbitrary",)),
    )(*args)


def _rwkv_scan_body(w_ref, kk_ref, b_ref, k_ref, r_ref, v_ref, s0_ref, y_ref, sT_ref, S, *, TB, NV):
    tb = pl.program_id(1)

    @pl.when(tb == 0)
    def _():
        S[...] = s0_ref[...]

    def step(t, carry):
        w, kk, bb, k, r = w_ref[t], kk_ref[t], b_ref[t], k_ref[t], r_ref[t]
        for vi in range(NV):
            s = S[vi]
            sa = jnp.sum(s * kk, axis=0, keepdims=True)
            vv = v_ref[t, pl.ds(vi, 1), :]
            s = s * w - sa * bb + vv * k
            S[vi] = s
            y_ref[t, pl.ds(vi, 1), :] = jnp.sum(s * r, axis=0, keepdims=True)
        return carry

    lax.fori_loop(0, TB, step, 0)

    @pl.when(tb == pl.num_programs(1) - 1)
    def _():
        sT_ref[...] = S[...]


def rwkv_scan(w, kk, b, k, r, v, s0, TB):
    T, _, LN = w.shape
    NV = v.shape[1]
    assert T % TB == 0 and LN % LANES == 0
    kspec = pl.BlockSpec((TB, HD, LANES), lambda c, t: (t, 0, c))
    vspec = pl.BlockSpec((TB, NV, LANES), lambda c, t: (t, 0, c))
    sspec = pl.BlockSpec((NV, HD, LANES), lambda c, t: (0, 0, c))
    return pl.pallas_call(
        functools.partial(_rwkv_scan_body, TB=TB, NV=NV),
        grid=(LN // LANES, T // TB),
        in_specs=[kspec] * 5 + [vspec, sspec],
        out_specs=[vspec, sspec],
        out_shape=[jax.ShapeDtypeStruct((T, NV, LN), F32), jax.ShapeDtypeStruct((NV, HD, LN), F32)],
        scratch_shapes=[pltpu.VMEM((NV, HD, LANES), F32)],
        compiler_params=_cparams(("arbitrary", "arbitrary")),
    )(w, kk, b, k, r, v, s0)


def _rwkv_post_body(y_ref, r_ref, k_ref, v_ref, g_ref, vec_ref, bd_ref, o_ref):
    bd = bd_ref[...]
    y = y_ref[...]
    r_k, ln_w, ln_b = vec_ref[4:5, :], vec_ref[5:6, :], vec_ref[6:7, :]
    mean = _segsum(y, bd) * (1.0 / HD)
    d = y - mean
    var = _segsum(d * d, bd) * (1.0 / HD)
    yn = d * lax.rsqrt(var + RWKV_GN_EPS) * ln_w + ln_b
    v = v_ref[...]
    bonus = _segsum(r_ref[...] * k_ref[...] * r_k, bd) * v
    o_ref[...] = ((yn + bonus) * g_ref[...]).astype(o_ref.dtype)


def rwkv_post(grp, y, r, kmod, v, g, vecs8, bd):
    T, D = y.shape
    tm = grp.tm
    tile = pl.BlockSpec((tm, D), lambda i: (i, 0))
    return pl.pallas_call(
        _rwkv_post_body,
        grid=(T // tm,),
        in_specs=[tile] * 5 + [pl.BlockSpec((8, D), lambda i: (0, 0)),
                               pl.BlockSpec((MXU_DIM, MXU_DIM), lambda i: (0, 0))],
        out_specs=tile,
        out_shape=jax.ShapeDtypeStruct((T, D), BF16),
        compiler_params=_cparams(("arbitrary",)),
    )(y, r, kmod, v, g, vecs8, bd)


def _ssd_pre_body(*refs, per_token, tiles_per_seq):
    if per_token:
        x_ref, c0_ref, c1_ref, c2_ref, cw_ref, cb_ref, sm_ref, dtb_ref, xs_o, bc_o, dt_o = refs
        x = x_ref[...]
        taps = [c0_ref[...], c1_ref[...], c2_ref[...], x]
    else:
        x_ref, cw_ref, cb_ref, sm_ref, dtb_ref, xs_o, bc_o, dt_o, carry = refs
        x = x_ref[...]
        tm = x.shape[0]

        @pl.when((pl.program_id(0) % tiles_per_seq) == 0)
        def _():
            carry[...] = jnp.zeros_like(carry)

        ext = jnp.concatenate([carry[...], x], axis=0)
        taps = [pltpu.roll(ext, CONV_W - 1 - i, 0)[SUBLANES:, :] for i in range(CONV_W - 1)] + [x]
        carry[...] = x[tm - SUBLANES:, :]
    conv = cb_ref[...] + taps[0] * cw_ref[0:1, :]
    for i in range(1, CONV_W):
        conv = conv + taps[i] * cw_ref[i:i + 1, :]
    xc = _silu(conv)
    xs_o[...] = xc[:, :SSM_DIM]
    bc_o[...] = xc[:, SSM_DIM:]
    dt_o[...] = _softplus(sm_ref[...] + dtb_ref[...])


def ssd_pre(grp, xbc, conv0, cw8, cb, zsm_dt, dtb):
    T = xbc.shape[0]
    tm = grp.tm
    tile = lambda n: pl.BlockSpec((tm, n), lambda i: (i, 0))
    full = lambda shp: pl.BlockSpec(shp, lambda i: tuple(0 for _ in shp))
    w_specs = [full((8, CONV_DIM)), full((1, CONV_DIM)), tile(LANES), full((1, LANES))]
    if grp.per_token:
        in_specs = [tile(CONV_DIM)] * 4 + w_specs
        args = [xbc, conv0[:, 0], conv0[:, 1], conv0[:, 2], cw8, cb, zsm_dt, dtb]
        scratch = []
        tiles_per_seq = 1
    else:
        in_specs = [tile(CONV_DIM)] + w_specs
        args = [xbc, cw8, cb, zsm_dt, dtb]
        scratch = [pltpu.VMEM((SUBLANES, CONV_DIM), F32)]
        tiles_per_seq = grp.L // tm
    return pl.pallas_call(
        functools.partial(_ssd_pre_body, per_token=grp.per_token, tiles_per_seq=tiles_per_seq),
        grid=(T // tm,),
        in_specs=in_specs,
        out_specs=[tile(SSM_DIM), tile(CONV_DIM - SSM_DIM), tile(LANES)],
        out_shape=[jax.ShapeDtypeStruct((T, SSM_DIM), F32),
                   jax.ShapeDtypeStruct((T, CONV_DIM - SSM_DIM), F32),
                   jax.ShapeDtypeStruct((T, LANES), F32)],
        scratch_shapes=scratch,
        compiler_params=_cparams(("arbitrary",)),
    )(*args)


def _ssd_chunk_body(xs_ref, bc_ref, dtc_ref, dtr_ref, ac_ref, ar_ref, ex_ref, y_ref, sT_ref, St):
    Q = SSD_CHUNK
    c = pl.program_id(1)

    @pl.when(c == 0)
    def _():
        St[...] = jnp.zeros_like(St)

    ri = lax.broadcasted_iota(jnp.int32, (Q, Q), 0)
    ci = lax.broadcasted_iota(jnp.int32, (Q, Q), 1)
    causal = ri >= ci
    tri = jnp.where(causal, 1.0, 0.0).astype(BF16)
    triT = jnp.where(ri <= ci, 1.0, 0.0).astype(BF16)
    dtc = dtc_ref[...]
    a_col = dtc * (-jnp.exp(ac_ref[...]))
    a_row = dtr_ref[...] * (-jnp.exp(ar_ref[...]))
    acs_col = _dot01_lhs(tri, a_col)
    acs_row = _dot01(a_row, triT)
    ex = ex_ref[...]
    dt_full = _dot01(dtc, ex)
    acs_full = _dot01(acs_col, ex)
    xs = xs_ref[...]
    xdt = xs * dt_full
    xdec = xdt * jnp.exp(acs_full[Q - 1:Q, :] - acs_full)
    eacs = jnp.exp(acs_full)
    lane = lax.broadcasted_iota(jnp.int32, (Q, LANES), 1)
    row = lax.broadcasted_iota(jnp.int32, (LANES, LANES), 0)
    xdt_b = xdt.astype(BF16)
    for g in range(SSM_GROUPS):
        bm = bc_ref[:, g * SSM_STATE:(g + 1) * SSM_STATE].astype(BF16)
        cm = bc_ref[:, (SSM_GROUPS + g) * SSM_STATE:(SSM_GROUPS + g + 1) * SSM_STATE].astype(BF16)
        cb = _dot_nt(cm, bm)
        for jp in range(HEADS // SSM_GROUPS // 2):
            j = g * (HEADS // SSM_GROUPS // 2) + jp
            sl = slice(j * LANES, (j + 1) * LANES)
            outs = []
            for hh in range(2):
                h = 2 * j + hh
                diff = acs_col[:, h:h + 1] - acs_row[h:h + 1, :]
                lm = jnp.where(causal, jnp.exp(jnp.minimum(diff, 0.0)), 0.0)
                outs.append(_dot((cb * lm).astype(BF16), xdt_b[:, sl]))
            y_diag = jnp.where(lane < HD, outs[0], outs[1])
            s_old = St[j]
            y_off = _dot_nt(cm, s_old.astype(BF16)) * eacs[:, sl]
            y_ref[:, sl] = y_diag + y_off
            new = _dot(jnp.transpose(xdec[:, sl]).astype(BF16), bm)
            tot0 = jnp.exp(acs_row[2 * j:2 * j + 1, Q - 1:Q])
            tot1 = jnp.exp(acs_row[2 * j + 1:2 * j + 2, Q - 1:Q])
            St[j] = s_old * jnp.where(row < HD, tot0, tot1) + new

    @pl.when(c == pl.num_programs(1) - 1)
    def _():
        sT_ref[...] = St[...]


def _dot01_lhs(m01, x):
    a, b, c = _split3(x)
    return _dot(m01, a) + _dot(m01, b) + _dot(m01, c)


def ssd_chunk(B, L, xs, bc, dtp, dtpT, alog_col, alog_row, expand):
    Q = SSD_CHUNK
    nc = L // Q
    return pl.pallas_call(
        _ssd_chunk_body,
        grid=(B, nc),
        in_specs=[pl.BlockSpec((Q, SSM_DIM), lambda b, c: (b * nc + c, 0)),
                  pl.BlockSpec((Q, CONV_DIM - SSM_DIM), lambda b, c: (b * nc + c, 0)),
                  pl.BlockSpec((Q, LANES), lambda b, c: (b * nc + c, 0)),
                  pl.BlockSpec((None, LANES, Q), lambda b, c: (b, 0, c)),
                  pl.BlockSpec((1, LANES), lambda b, c: (0, 0)),
                  pl.BlockSpec((LANES, 1), lambda b, c: (0, 0)),
                  pl.BlockSpec((LANES, SSM_DIM), lambda b, c: (0, 0))],
        out_specs=[pl.BlockSpec((Q, SSM_DIM), lambda b, c: (b * nc + c, 0)),
                   pl.BlockSpec((None, HEADS // 2, LANES, SSM_STATE), lambda b, c: (b, 0, 0, 0))],
        out_shape=[jax.ShapeDtypeStruct((B * L, SSM_DIM), F32),
                   jax.ShapeDtypeStruct((B, HEADS // 2, LANES, SSM_STATE), F32)],
        scratch_shapes=[pltpu.VMEM((HEADS // 2, LANES, SSM_STATE), F32)],
        compiler_params=_cparams(("arbitrary", "arbitrary")),
    )(xs, bc, dtp, dtpT, alog_row, alog_col, expand)


def _ssd_step_body(h_ref, x_ref, b_ref, c_ref, dt_ref, al_ref, y_ref, hn_ref):
    dt = dt_ref[...]
    dA = jnp.exp(dt * (-jnp.exp(al_ref[...])))
    bm, cm = b_ref[...], c_ref[...]
    for p in range(HD):
        hp = h_ref[p] * dA + (x_ref[p:p + 1, :] * dt) * bm
        hn_ref[p] = hp
        y_ref[p:p + 1, :] = jnp.sum(hp * cm, axis=0, keepdims=True)


def ssd_step(h0, x, bm, cm, dt, alog):
    LN = h0.shape[-1]
    lt = lambda n: pl.BlockSpec((n, LANES), lambda c: (0, c))
    hs = pl.BlockSpec((HD, SSM_STATE, LANES), lambda c: (0, 0, c))
    return pl.pallas_call(
        _ssd_step_body,
        grid=(LN // LANES,),
        in_specs=[hs, lt(HD), lt(SSM_STATE), lt(SSM_STATE), lt(1), lt(1)],
        out_specs=[lt(HD), hs],
        out_shape=[jax.ShapeDtypeStruct((HD, LN), F32), jax.ShapeDtypeStruct((HD, SSM_STATE, LN), F32)],
        compiler_params=_cparams(("arbitrary",)),
    )(h0, x, bm, cm, dt, alog)


def _ssd_post_body(y_ref, xs_ref, zg_ref, d_ref, nw_ref, o_ref):
    y = (y_ref[...] + d_ref[...] * xs_ref[...]) * _silu(zg_ref[...])
    gs = SSM_DIM // SSM_GROUPS
    for g in range(SSM_GROUPS):
        sl = slice(g * gs, (g + 1) * gs)
        yg = y[:, sl]
        n = yg * lax.rsqrt(jnp.mean(yg * yg, axis=-1, keepdims=True) + NORM_EPS)
        o_ref[:, sl] = (n * nw_ref[:, sl]).astype(o_ref.dtype)


def ssd_post(grp, y, xs, zg, dfull, nw):
    T, D = y.shape
    tm = grp.tm
    tile = pl.BlockSpec((tm, D), lambda i: (i, 0))
    row = pl.BlockSpec((1, D), lambda i: (0, 0))
    return pl.pallas_call(
        _ssd_post_body,
        grid=(T // tm,),
        in_specs=[tile, tile, tile, row, row],
        out_specs=tile,
        out_shape=jax.ShapeDtypeStruct((T, D), BF16),
        compiler_params=_cparams(("arbitrary",)),
    )(y, xs, zg, dfull, nw)


def _mla_prep_body(zq_ref, zsm_ref, cs_ref, qn_ref, kvn_ref, wq_ref, wkv_ref, gq_ref, gk_ref,
                   q_o, k_o, v_o, lat_o, kpe_o, wq_b, wkv_b):
    @pl.when(pl.program_id(0) == 0)
    def _():
        wq_b[...] = wq_ref[...].astype(BF16)
        wkv_b[...] = wkv_ref[...].astype(BF16)

    HP = HEADS * LANES
    zq = zq_ref[...]
    qd, kvd = zq[:, :Q_LORA], zq[:, Q_LORA:]
    qn = qd * lax.rsqrt(jnp.mean(qd * qd, axis=-1, keepdims=True) + NORM_EPS) * qn_ref[...]
    lat = kvd * lax.rsqrt(jnp.mean(kvd * kvd, axis=-1, keepdims=True) + NORM_EPS) * kvn_ref[...]
    lat_o[...] = lat
    cosv, sinv = cs_ref[:, :LANES], cs_ref[:, LANES:]
    kpe = zsm_ref[:, :LANES] * cosv + zsm_ref[:, LANES:2 * LANES] * sinv
    kpe_o[...] = kpe
    qq = _dot(qn.astype(BF16), wq_b[...])
    kv = _dot(lat.astype(BF16), wkv_b[...])
    v_o[...] = kv[:, HP:].astype(v_o.dtype)
    gq, gk = gq_ref[...], gk_ref[...]
    nope_one = jnp.where(lax.broadcasted_iota(jnp.int32, (1, LANES), 1) < NOPE_DIM, 1.0, 0.0)
    cq = cosv + nope_one
    for h in range(HEADS):
        sl = slice(h * LANES, (h + 1) * LANES)
        qh = qq[:, sl] * cq + qq[:, HP + h * LANES:HP + (h + 1) * LANES] * sinv
        rq = lax.rsqrt(jnp.sum(qh * qh, axis=-1, keepdims=True) * (1.0 / QK_DIM) + NORM_EPS)
        q_o[:, sl] = (qh * rq * gq).astype(q_o.dtype)
        kh = kv[:, sl] + kpe
        rk = lax.rsqrt(jnp.sum(kh * kh, axis=-1, keepdims=True) * (1.0 / QK_DIM) + NORM_EPS)
        k_o[:, sl] = (kh * rk * gk).astype(k_o.dtype)


def mla_prep(grp, zq, zsm_kpe, cs, qn, kvn, wq2, wkv, gq, gk):
    T = zq.shape[0]
    tm = grp.tm
    HP = HEADS * LANES
    tile = lambda n: pl.BlockSpec((tm, n), lambda i: (i, 0))
    full = lambda shp: pl.BlockSpec(shp, lambda i: tuple(0 for _ in shp))
    if grp.per_token:
        cs_spec = full((1, 2 * LANES))
    else:
        tps = grp.L // tm
        cs_spec = pl.BlockSpec((tm, 2 * LANES), lambda i: (i % tps, 0))
    return pl.pallas_call(
        _mla_prep_body,
        grid=(T // tm,),
        in_specs=[tile(Q_LORA + KV_LORA), tile(2 * LANES), cs_spec, full((1, Q_LORA)), full((1, KV_LORA)),
                  full((Q_LORA, 2 * HP)), full((KV_LORA, HP + HEADS * HD)), full((1, LANES)), full((1, LANES))],
        out_specs=[tile(HP), tile(HP), tile(HEADS * HD), tile(KV_LORA), tile(LANES)],
        out_shape=[jax.ShapeDtypeStruct((T, HP), BF16), jax.ShapeDtypeStruct((T, HP), BF16),
                   jax.ShapeDtypeStruct((T, HEADS * HD), BF16), jax.ShapeDtypeStruct((T, KV_LORA), F32),
                   jax.ShapeDtypeStruct((T, LANES), F32)],
        scratch_shapes=[pltpu.VMEM((Q_LORA, 2 * HP), BF16), pltpu.VMEM((KV_LORA, HP + HEADS * HD), BF16)],
        compiler_params=_cparams(("arbitrary",)),
    )(zq, zsm_kpe, cs, qn, kvn, wq2, wkv, gq, gk)


def _flash_body(q_ref, k_ref, v_ref, o_ref, *, tq):
    qi = pl.program_id(2)
    lane = lax.broadcasted_iota(jnp.int32, (tq, LANES), 1)
    ri = lax.broadcasted_iota(jnp.int32, (tq, tq), 0)
    ci = lax.broadcasted_iota(jnp.int32, (tq, tq), 1)
    qs = [q_ref[:, hh * LANES:(hh + 1) * LANES] for hh in range(2)]

    def block(ki, carry, masked):
        off = pl.multiple_of(ki * tq, tq)
        kb = k_ref[pl.ds(off, tq), :]
        vb = v_ref[pl.ds(off, tq), :]
        out = []
        for hh in range(2):
            m, l, acc = carry[hh]
            s = _dot_nt(qs[hh], kb[:, hh * LANES:(hh + 1) * LANES])
            if masked:
                s = jnp.where(ci <= ri, s, NEG_INF)
            m_new = jnp.maximum(m, jnp.max(s, axis=-1, keepdims=True))
            corr = jnp.exp(m - m_new)
            p = jnp.exp(s - m_new)
            l = l * corr + jnp.sum(p, axis=-1, keepdims=True)
            acc = acc * corr + _dot(p.astype(BF16), vb)
            out.append((m_new, l, acc))
        return tuple(out)

    init = tuple((jnp.full((tq, 1), NEG_INF, F32), jnp.zeros((tq, 1), F32), jnp.zeros((tq, LANES), F32))
                 for _ in range(2))
    carry = lax.fori_loop(0, qi, lambda ki, c: block(ki, c, False), init)
    carry = block(qi, carry, True)
    o0 = carry[0][2] / carry[0][1]
    o1 = carry[1][2] / carry[1][1]
    o_ref[...] = jnp.where(lane < HD, o0, o1).astype(o_ref.dtype)


def flash_causal(B, L, q, k, v, tq):
    nq = L // tq
    PW = 2 * LANES
    return pl.pallas_call(
        functools.partial(_flash_body, tq=tq),
        grid=(B, HEADS // 2, nq),
        in_specs=[pl.BlockSpec((tq, PW), lambda b, hp, i: (b * nq + i, hp)),
                  pl.BlockSpec((L, PW), lambda b, hp, i: (b, hp)),
                  pl.BlockSpec((L, LANES), lambda b, hp, i: (b, hp))],
        out_specs=pl.BlockSpec((tq, LANES), lambda b, hp, i: (b * nq + i, hp)),
        out_shape=jax.ShapeDtypeStruct((B * L, HEADS * HD), BF16),
        compiler_params=_cparams(("arbitrary", "arbitrary", "arbitrary")),
    )(q, k, v)


def _absorb_body(q_ref, wkv_ref, gk_ref, pm_ref, ql_o, qr_o):
    q = q_ref[...].astype(F32) * gk_ref[...]
    lane = lax.broadcasted_iota(jnp.int32, q.shape, 1)
    qn = jnp.where(lane < NOPE_DIM, q, 0.0).astype(BF16)
    ql_o[...] = _dot_nt(qn, wkv_ref[...].astype(BF16)).astype(ql_o.dtype)
    qr_o[...] = _dot(q.astype(BF16), pm_ref[...]).astype(qr_o.dtype)


def mla_absorb(qpad, wkv2d, gk, pm):
    Bd = qpad.shape[0]
    return pl.pallas_call(
        _absorb_body,
        grid=(HEADS,),
        in_specs=[pl.BlockSpec((Bd, LANES), lambda h: (0, h)),
                  pl.BlockSpec((KV_LORA, LANES), lambda h: (0, h)),
                  pl.BlockSpec((1, LANES), lambda h: (0, 0)),
                  pl.BlockSpec((LANES, ROPE_DIM), lambda h: (0, 0))],
        out_specs=[pl.BlockSpec((None, Bd, KV_LORA), lambda h: (h, 0, 0)),
                   pl.BlockSpec((None, Bd, ROPE_DIM), lambda h: (h, 0, 0))],
        out_shape=[jax.ShapeDtypeStruct((HEADS, Bd, KV_LORA), BF16),
                   jax.ShapeDtypeStruct((HEADS, Bd, ROPE_DIM), BF16)],
        compiler_params=_cparams(("arbitrary",)),
    )(qpad, wkv2d, gk, pm)


def _decode_body(pt_ref, *refs, G):
    lat_refs = refs[:G]
    kpe_refs = refs[G:2 * G]
    ql_ref, qr_ref, wk_ref, seg_ref, nlat_ref, nkpe_ref, o_ref, m_s, l_s, acc_s, wk_b = refs[2 * G:]
    g = pl.program_id(1)

    @pl.when(g == 0)
    def _():
        m_s[...] = jnp.full_like(m_s, NEG_INF)
        l_s[...] = jnp.zeros_like(l_s)
        acc_s[...] = jnp.zeros_like(acc_s)

    @pl.when(jnp.logical_and(pl.program_id(0) == 0, g == 0))
    def _():
        wk_b[...] = wk_ref[...].astype(BF16)

    ql = ql_ref[...]
    qr = qr_ref[...]
    seg = seg_ref[...]
    ones_r = jnp.ones((HEADS, ROPE_DIM), BF16)

    def merge(lat, kpe, valid):
        lat_b = lat.astype(BF16)
        kpe_b = kpe.astype(BF16)
        kn = _dot(lat_b, wk_b[...])
        sq = kn * kn
        sq8 = sq[:, :LANES]
        for c in range(1, HEADS * HD // LANES):
            sq8 = sq8 + sq[:, c * LANES:(c + 1) * LANES]
        ssq = _dot_nt(seg, sq8.astype(BF16)) + _dot_nt(ones_r, (kpe * kpe).astype(BF16))
        rinv = lax.rsqrt(ssq * (1.0 / QK_DIM) + NORM_EPS)
        s = (_dot_nt(ql, lat_b) + _dot_nt(qr, kpe_b)) * rinv
        if valid is not None:
            s = jnp.where(valid, s, NEG_INF)
        m = m_s[...]
        m_new = jnp.maximum(m, jnp.max(s, axis=-1, keepdims=True))
        corr = jnp.exp(m - m_new)
        p = jnp.exp(s - m_new)
        l_s[...] = l_s[...] * corr + jnp.sum(p, axis=-1, keepdims=True)
        acc_s[...] = acc_s[...] * corr + _dot(p.astype(BF16), lat_b)
        m_s[...] = m_new

    for j in range(0, G, 2):
        lat = jnp.concatenate([lat_refs[j][...], lat_refs[j + 1][...]], axis=0)
        kpe = jnp.concatenate([kpe_refs[j][...], kpe_refs[j + 1][...]], axis=0)
        merge(lat, kpe, None)

    @pl.when(g == pl.num_programs(1) - 1)
    def _():
        col = lax.broadcasted_iota(jnp.int32, (HEADS, PAGE), 1)
        merge(nlat_ref[...], nkpe_ref[...], col == 0)
        o_ref[...] = acc_s[...] / l_s[...]


def mla_decode(l, cache_lat, cache_kpe, page_table, ql, qr, wk_perm, seg, nlat8, nkpe8, G):
    Bd, n_pages = page_table.shape
    assert n_pages % G == 0 and G % 2 == 0
    lat_specs = [pl.BlockSpec((None, None, PAGE, KV_LORA),
                              functools.partial(lambda b, g, pt, j: (l, pt[b, g * G + j], 0, 0), j=j))
                 for j in range(G)]
    kpe_specs = [pl.BlockSpec((None, None, PAGE, ROPE_DIM),
                              functools.partial(lambda b, g, pt, j: (l, pt[b, g * G + j], 0, 0), j=j))
                 for j in range(G)]
    bspec = lambda r, c: pl.BlockSpec((None, r, c), lambda b, g, pt: (b, 0, 0))
    grid_spec = pltpu.PrefetchScalarGridSpec(
        num_scalar_prefetch=1,
        grid=(Bd, n_pages // G),
        in_specs=lat_specs + kpe_specs + [
            bspec(HEADS, KV_LORA), bspec(HEADS, ROPE_DIM),
            pl.BlockSpec((KV_LORA, HEADS * HD), lambda b, g, pt: (0, 0)),
            pl.BlockSpec((HEADS, LANES), lambda b, g, pt: (0, 0)),
            bspec(PAGE, KV_LORA), bspec(PAGE, ROPE_DIM)],
        out_specs=bspec(HEADS, KV_LORA),
        scratch_shapes=[pltpu.VMEM((HEADS, 1), F32), pltpu.VMEM((HEADS, 1), F32),
                        pltpu.VMEM((HEADS, KV_LORA), F32), pltpu.VMEM((KV_LORA, HEADS * HD), BF16)],
    )
    return pl.pallas_call(
        functools.partial(_decode_body, G=G),
        grid_spec=grid_spec,
        out_shape=jax.ShapeDtypeStruct((Bd, HEADS, KV_LORA), F32),
        compiler_params=_cparams(("arbitrary", "arbitrary")),
    )(page_table, *([cache_lat] * G), *([cache_kpe] * G), ql, qr, wk_perm, seg, nlat8, nkpe8)


def _vup_body(a_ref, w_ref, o_ref):
    w = w_ref[...].astype(BF16)
    lane = lax.broadcasted_iota(jnp.int32, o_ref.shape, 1)
    o0 = _dot(a_ref[0].astype(BF16), w[:, :LANES])
    o1 = _dot(a_ref[1].astype(BF16), w[:, LANES:])
    o0 = pltpu.roll(o0, HD, 1)
    o_ref[...] = jnp.where(lane < HD, o0, o1).astype(o_ref.dtype)


def mla_vup(acc_h, wkv2d):
    Bd = acc_h.shape[1]
    return pl.pallas_call(
        _vup_body,
        grid=(HEADS // 2,),
        in_specs=[pl.BlockSpec((2, Bd, KV_LORA), lambda j: (j, 0, 0)),
                  pl.BlockSpec((KV_LORA, 2 * LANES), lambda j: (0, j))],
        out_specs=pl.BlockSpec((Bd, LANES), lambda j: (0, j)),
        out_shape=jax.ShapeDtypeStruct((Bd, HEADS * HD), BF16),
        compiler_params=_cparams(("arbitrary",)),
    )(acc_h, wkv2d)


def _merge_body(ya_ref, yb_ref, yc_ref, w_ref, g0_ref, g1_ref, g2_ref, o_ref, wb):
    @pl.when(pl.program_id(1) == 0)
    def _():
        wb[...] = w_ref[...].astype(BF16)

    acc = g0_ref[...].astype(F32) * _dot(ya_ref[...], wb[0])
    acc = acc + g1_ref[...].astype(F32) * _dot(yb_ref[...], wb[1])
    acc = acc + g2_ref[...].astype(F32) * _dot(yc_ref[...], wb[2])
    o_ref[...] = acc.astype(o_ref.dtype)


def merge_mm(ya, yb, yc, wb, gates, tm, tn):
    T, K = ya.shape
    D = wb.shape[2]
    yspec = pl.BlockSpec((tm, K), lambda j, i: (i, 0))
    return pl.pallas_call(
        _merge_body,
        grid=(D // tn, T // tm),
        in_specs=[yspec, yspec, yspec,
                  pl.BlockSpec((N_BRANCH, K, tn), lambda j, i: (0, 0, j))
                  ] + [pl.BlockSpec((tm, tn), functools.partial(lambda j, i, b: (i, b * (D // tn) + j), b=b))
                       for b in range(N_BRANCH)],
        out_specs=pl.BlockSpec((tm, tn), lambda j, i: (i, j)),
        out_shape=jax.ShapeDtypeStruct((T, D), BF16),
        scratch_shapes=[pltpu.VMEM((N_BRANCH, K, tn), BF16)],
        compiler_params=_cparams(("arbitrary", "arbitrary")),
    )(ya, yb, yc, wb, gates, gates, gates)


def _lane_partner(x, sh, lane):
    up = pltpu.roll(x, LANES - sh, 1)
    dn = pltpu.roll(x, sh, 1)
    return jnp.where((lane & sh) == 0, up, dn)


def _seg8(x, op, lane):
    for sh in (1, 2, 4):
        x = op(x, _lane_partner(x, sh, lane))
    return x


def _route_body(lg_ref, bias_ref, gate_o, idx_o):
    lg = lg_ref[...]
    shape = lg.shape
    lane = lax.broadcasted_iota(jnp.int32, shape, 1)
    valid = lane < N_EXPERTS
    s = _sigmoid(lg)
    NINF = -jnp.inf
    sb = jnp.where(valid, s + bias_ref[...], NINF)
    BIG = jnp.int32(1 << 20)
    m1 = _seg8(sb, jnp.maximum, lane)
    i1 = _seg8(jnp.where(sb == m1, lane, BIG), jnp.minimum, lane)
    m2 = _seg8(jnp.where(lane == i1, NINF, sb), jnp.maximum, lane)
    gs = jnp.where(valid, m1 + m2, NINF)
    grp = lane >> 3
    gsel = jnp.zeros(shape, jnp.bool_)
    for _ in range(TOPK_GROUPS):
        mx = jnp.max(gs, axis=-1, keepdims=True)
        first = jnp.min(jnp.where(gs == mx, lane, BIG), axis=-1, keepdims=True)
        hit = grp == (first >> 3)
        gsel = jnp.logical_or(gsel, hit)
        gs = jnp.where(hit, NINF, gs)
    cand = jnp.where(valid, jnp.where(gsel, sb, NEG_INF), NINF)
    sel = jnp.zeros(shape, jnp.bool_)
    idx = jnp.zeros(shape, jnp.int32)
    for it in range(TOP_K):
        mx = jnp.max(cand, axis=-1, keepdims=True)
        first = jnp.min(jnp.where(cand == mx, lane, BIG), axis=-1, keepdims=True)
        hit = lane == first
        sel = jnp.logical_or(sel, hit)
        cand = jnp.where(hit, NINF, cand)
        idx = jnp.where(lane == it, first, idx)
    w = jnp.where(sel, s, 0.0)
    gate_o[...] = w / jnp.sum(w, axis=-1, keepdims=True) * ROUTED_SCALE
    idx_o[...] = idx


def route(logits, bias_pad, tm):
    T = logits.shape[0]
    tile = pl.BlockSpec((tm, LANES), lambda i: (i, 0))
    return pl.pallas_call(
        _route_body,
        grid=(T // tm,),
        in_specs=[tile, pl.BlockSpec((1, LANES), lambda i: (0, 0))],
        out_specs=[tile, tile],
        out_shape=[jax.ShapeDtypeStruct((T, LANES), F32), jax.ShapeDtypeStruct((T, LANES), jnp.int32)],
        compiler_params=_cparams(("arbitrary",)),
    )(logits, bias_pad)


def _expert_body(te_ref, x_ref, wg_ref, wu_ref, wd_ref, rw_ref, o_ref, wg_b, wu_b, wd_b):
    i = pl.program_id(0)
    prev = te_ref[jnp.maximum(i - 1, 0)]

    @pl.when(jnp.logical_or(i == 0, te_ref[i] != prev))
    def _():
        wg_b[...] = wg_ref[...].astype(BF16)
        wu_b[...] = wu_ref[...].astype(BF16)
        wd_b[...] = wd_ref[...].astype(BF16)

    x = x_ref[...]
    hid = _silu(_dot(x, wg_b[...])) * _dot(x, wu_b[...])
    y = _dot(hid.astype(BF16), wd_b[...])
    o_ref[...] = (y * rw_ref[...]).astype(o_ref.dtype)


def expert_ffn(l, tile_expert, xs, w_gate, w_up, w_down, row_w, tm):
    M, D = xs.shape
    FF = w_gate.shape[-1]
    grid_spec = pltpu.PrefetchScalarGridSpec(
        num_scalar_prefetch=1,
        grid=(M // tm,),
        in_specs=[pl.BlockSpec((tm, D), lambda i, te: (i, 0)),
                  pl.BlockSpec((None, None, D, FF), lambda i, te: (l, te[i], 0, 0)),
                  pl.BlockSpec((None, None, D, FF), lambda i, te: (l, te[i], 0, 0)),
                  pl.BlockSpec((None, None, FF, D), lambda i, te: (l, te[i], 0, 0)),
                  pl.BlockSpec((tm, 1), lambda i, te: (i, 0))],
        out_specs=pl.BlockSpec((tm, D), lambda i, te: (i, 0)),
        scratch_shapes=[pltpu.VMEM((D, FF), BF16), pltpu.VMEM((D, FF), BF16), pltpu.VMEM((FF, D), BF16)],
    )
    return pl.pallas_call(
        _expert_body,
        grid_spec=grid_spec,
        out_shape=jax.ShapeDtypeStruct((M, D), BF16),
        compiler_params=_cparams(("arbitrary",)),
    )(tile_expert, xs, w_gate, w_up, w_down, row_w)


def _glu_body(x_ref, wg_ref, wu_ref, o_ref, wg_b, wu_b):
    @pl.when(pl.program_id(0) == 0)
    def _():
        wg_b[...] = wg_ref[...].astype(BF16)
        wu_b[...] = wu_ref[...].astype(BF16)

    x = x_ref[...]
    o_ref[...] = (_silu(_dot(x, wg_b[...])) * _dot(x, wu_b[...])).astype(o_ref.dtype)


def glu_mm(x, wg, wu, tm):
    T, D = x.shape
    FF = wg.shape[1]
    return pl.pallas_call(
        _glu_body,
        grid=(T // tm,),
        in_specs=[pl.BlockSpec((tm, D), lambda i: (i, 0)),
                  pl.BlockSpec((D, FF), lambda i: (0, 0)), pl.BlockSpec((D, FF), lambda i: (0, 0))],
        out_specs=pl.BlockSpec((tm, FF), lambda i: (i, 0)),
        out_shape=jax.ShapeDtypeStruct((T, FF), BF16),
        scratch_shapes=[pltpu.VMEM((D, FF), BF16), pltpu.VMEM((D, FF), BF16)],
        compiler_params=_cparams(("arbitrary",)),
    )(x, wg, wu)


def _combine_body(y_ref, o_ref):
    D = o_ref.shape[1]
    acc = y_ref[:, 0:D].astype(F32)
    for k in range(1, TOP_K):
        acc = acc + y_ref[:, k * D:(k + 1) * D].astype(F32)
    o_ref[...] = acc.astype(o_ref.dtype)


def combine_topk(y8, tm):
    T = y8.shape[0]
    D = y8.shape[1] // TOP_K
    return pl.pallas_call(
        _combine_body,
        grid=(T // tm,),
        in_specs=[pl.BlockSpec((tm, TOP_K * D), lambda i: (i, 0))],
        out_specs=pl.BlockSpec((tm, D), lambda i: (i, 0)),
        out_shape=jax.ShapeDtypeStruct((T, D), BF16),
        compiler_params=_cparams(("arbitrary",)),
    )(y8)


def _rope_tables(pos):
    inv = 1.0 / (ROPE_THETA ** (jnp.arange(0, ROPE_DIM, 2, dtype=F32) / ROPE_DIM))
    ang = pos.astype(F32)[:, None] * inv[None, :]
    ang = jnp.concatenate([ang, ang], axis=-1)
    cos, sin = jnp.cos(ang), jnp.sin(ang)
    n = pos.shape[0]
    z64 = jnp.zeros((n, NOPE_DIM), F32)
    z32 = jnp.zeros((n, LANES - QK_DIM), F32)
    return jnp.concatenate([z64, cos, z32, z64, sin, z32], axis=-1)


def _rotate_half_cols(w):
    half = ROPE_DIM // 2
    return jnp.concatenate([-w[..., half:], w[..., :half]], axis=-1)


def _qk_gain_pad(g):
    full = jnp.concatenate([g, g[NOPE_DIM:]], axis=-1)
    return jnp.pad(full, (0, LANES - QK_DIM)).reshape(1, LANES)


def _layer_weights(l, W):
    D = D_MODEL
    o = {}
    w_in = W["w_in"][l]
    c0 = 0
    o["wA"] = w_in[:, c0:c0 + RWKV_IN]; c0 += RWKV_IN
    o["wZ"] = w_in[:, c0:c0 + SSM_DIM]; c0 += SSM_DIM
    o["wX"] = w_in[:, c0:c0 + CONV_DIM]; c0 += CONV_DIM
    w_dt = w_in[:, c0:c0 + HEADS]; c0 += HEADS
    o["wQKV"] = w_in[:, c0:c0 + Q_LORA + KV_LORA]; c0 += Q_LORA + KV_LORA
    w_kpe = w_in[:, c0:c0 + ROPE_DIM]; c0 += ROPE_DIM
    o["wG"] = w_in[:, c0:]
    zpad = lambda n: jnp.zeros((D, n), F32)
    o["wS"] = jnp.concatenate([zpad(NOPE_DIM), w_kpe, zpad(LANES - QK_DIM),
                               zpad(NOPE_DIM), _rotate_half_cols(w_kpe), zpad(LANES - QK_DIM),
                               w_dt, zpad(LANES - HEADS)], axis=1)
    o["mu"] = W["rwkv_mu"][l].reshape(1, RWKV_IN)
    o["vecs8"] = jnp.pad(W["rwkv_vecs"][l], ((0, 1), (0, 0)))
    zl = jnp.zeros((W_LORA, RWKV_DIM), F32)
    o["wup"] = jnp.concatenate([W["rwkv_w_up"][l], zl], axis=0).astype(BF16)
    o["aup"] = jnp.concatenate([zl, W["rwkv_a_up"][l]], axis=0).astype(BF16)
    o["gup"] = W["rwkv_g_up"][l].astype(BF16)
    if l > 0:
        o["v0"] = W["rwkv_v0"][l - 1].reshape(1, RWKV_DIM)
        o["vdn"] = jnp.pad(W["rwkv_v_down"][l - 1], ((0, 0), (0, LANES - V_LORA))).astype(BF16)
        o["vup"] = jnp.pad(W["rwkv_v_up"][l - 1], ((0, LANES - V_LORA), (0, 0))).astype(BF16)
    o["cw8"] = jnp.pad(W["ssm_conv_w"][l], ((0, 8 - CONV_W), (0, 0)))
    o["cb"] = W["ssm_conv_b"][l].reshape(1, CONV_DIM)
    o["dtb"] = jnp.pad(W["ssm_dt_bias"][l], (0, LANES - HEADS)).reshape(1, LANES)
    alog = jnp.pad(W["ssm_a_log"][l], (0, LANES - HEADS))
    o["alog_row"] = alog.reshape(1, LANES)
    o["alog_col"] = alog.reshape(LANES, 1)
    o["dfull"] = jnp.repeat(W["ssm_d"][l], HD).reshape(1, SSM_DIM)
    o["nw"] = W["ssm_norm_w"][l].reshape(1, SSM_DIM)
    wq = W["mla_q_up"][l]
    padh = lambda a: jnp.pad(a, ((0, 0), (0, 0), (0, LANES - a.shape[-1])))
    wq_rot = jnp.concatenate([jnp.zeros_like(wq[..., :NOPE_DIM]), _rotate_half_cols(wq[..., NOPE_DIM:])], axis=-1)
    o["wq2"] = jnp.concatenate([padh(wq).reshape(Q_LORA, -1), padh(wq_rot).reshape(Q_LORA, -1)], axis=1)
    wkv = W["mla_kv_up"][l]
    o["wkv_prep"] = jnp.concatenate([padh(wkv[..., :NOPE_DIM]).reshape(KV_LORA, -1),
                                     wkv[..., NOPE_DIM:].reshape(KV_LORA, -1)], axis=1)
    o["wkv2d"] = wkv.reshape(KV_LORA, HEADS * 2 * HD)
    wk = wkv[..., :NOPE_DIM].reshape(KV_LORA, HEADS, NOPE_DIM // 8, 8)
    o["wk_perm"] = jnp.transpose(wk, (0, 2, 1, 3)).reshape(KV_LORA, HEADS * NOPE_DIM)
    o["qn"] = W["mla_q_norm"][l].reshape(1, Q_LORA)
    o["kvn"] = W["mla_kv_norm"][l].reshape(1, KV_LORA)
    o["gq"] = _qk_gain_pad(W["mla_qk_gain_q"][l]) * ATTN_SCALE
    o["gk"] = _qk_gain_pad(W["mla_qk_gain_k"][l])
    o["w_router"] = jnp.pad(W["w_router"][l], ((0, 0), (0, LANES - N_EXPERTS)))
    o["router_bias"] = jnp.pad(W["router_bias"][l], (0, LANES - N_EXPERTS)).reshape(1, LANES)
    return o


def _chain_layout(a, B, L, ns):
    a = a.reshape(B, L, HEADS, HD)
    a = jnp.transpose(a, (1, 3, 0, 2)).reshape(L, HD, B * HEADS)
    return jnp.tile(a, (1, 1, ns)) if ns > 1 else a


def _mix_block(grp, l, x, mods, st, lw, W, consts, v_first, attend):
    B, L, T, tm = grp.B, grp.L, grp.T, grp.tm
    D = D_MODEL
    sh1, sc1, g1 = mods[0], mods[1], mods[2]
    h = norm_mod(grp, x, W["norm_mix"][l], sc1, sh1)
    zA = mm(h, lw["wA"], tm, RWKV_IN // 2)
    zg = mm(h, lw["wZ"], tm, SSM_DIM)
    xbc = mm(h, lw["wX"], tm, CONV_DIM // 2)
    zq = mm(h, lw["wQKV"], tm, Q_LORA + KV_LORA)
    zsm = mm(h, lw["wS"], tm, 3 * LANES)
    gates = mm(h, lw["wG"], tm, D // 2, epi="sigmoid", out_dtype=BF16)

    shift0, wkv0, conv0, ssm0 = st
    if grp.per_token:
        prev = shift0
        shift_new = zA
    else:
        prev = jnp.zeros((B, 1, RWKV_IN), F32)
        shift_new = zA.reshape(B, L, RWKV_IN)[:, -1]
    l1 = None if l == 0 else (lw["v0"], lw["vdn"], lw["vup"], v_first)
    r, dec, kmod, v, kk, bb, g = rwkv_prep(grp, zA, prev, lw["mu"], lw["vecs8"], lw["wup"], lw["aup"],
                                           lw["gup"], consts["bd64"], l1)
    if l == 0:
        v_first = v
    nchain = B * HEADS
    ns = max(1, LANES // nchain)
    NV = HD // ns
    lay = lambda a: _chain_layout(a, B, L, ns)
    vv = jnp.transpose(v.reshape(B, L, HEADS, ns, NV), (1, 4, 3, 0, 2)).reshape(L, NV, ns * nchain)
    if grp.per_token:
        s0 = jnp.transpose(wkv0.reshape(nchain, HD, HD), (1, 2, 0))
    else:
        s0 = jnp.zeros((NV, HD, ns * nchain), F32)
    TB = 1 if L == 1 else min(L, SCAN_TB)
    y, sT = rwkv_scan(lay(dec), lay(kk), lay(bb), lay(kmod), lay(r), vv, s0, TB)
    y = jnp.transpose(y.reshape(L, NV, ns, B, HEADS), (3, 0, 4, 2, 1)).reshape(T, RWKV_DIM)
    wkv_new = jnp.transpose(sT.reshape(NV, HD, ns, B, HEADS), (3, 4, 2, 0, 1)).reshape(B, HEADS, HD, HD)
    ya = rwkv_post(grp, y, r, kmod, v, g, lw["vecs8"], consts["bd64"])

    xs, bc, dtp = ssd_pre(grp, xbc, conv0, lw["cw8"], lw["cb"], zsm[:, 2 * LANES:], lw["dtb"])
    if grp.per_token:
        conv_new = jnp.stack([conv0[:, 1], conv0[:, 2], xbc], axis=1)
        h0 = jnp.transpose(ssm0.reshape(nchain, HD, SSM_STATE), (1, 2, 0))
        rep = lambda m: jnp.transpose(jnp.repeat(m.reshape(B, SSM_GROUPS, SSM_STATE), HEADS // SSM_GROUPS, axis=1)
                                      .reshape(nchain, SSM_STATE))
        ysd, hn = ssd_step(h0, jnp.transpose(xs.reshape(nchain, HD)), rep(bc[:, :SSM_GROUPS * SSM_STATE]),
                           rep(bc[:, SSM_GROUPS * SSM_STATE:]), dtp[:, :HEADS].reshape(1, nchain),
                           jnp.tile(lw["alog_row"][:, :HEADS], (1, B)))
        ysd = jnp.transpose(ysd).reshape(T, SSM_DIM)
        ssm_new = jnp.transpose(hn, (2, 0, 1)).reshape(B, HEADS, HD, SSM_STATE)
    else:
        conv_new = xbc.reshape(B, L, CONV_DIM)[:, L - (CONV_W - 1):]
        dtpT = jnp.transpose(dtp.reshape(B, L, LANES), (0, 2, 1))
        ysd, sfin = ssd_chunk(B, L, xs, bc, dtp, dtpT, lw["alog_col"], lw["alog_row"], consts["expand"])
        ssm_new = sfin.reshape(B, HEADS, HD, SSM_STATE)
    yb = ssd_post(grp, ysd, xs, zg, lw["dfull"], lw["nw"])

    q, k, vh, lat, kpe = mla_prep(grp, zq, zsm[:, :2 * LANES], consts["cs"], lw["qn"], lw["kvn"],
                                  lw["wq2"], lw["wkv_prep"], lw["gq"], lw["gk"])
    yc = attend(l, q, k, vh, lat, kpe, lw)
    lat_out = lat.reshape(B, L, KV_LORA)
    kpe_out = kpe[:, NOPE_DIM:QK_DIM].reshape(B, L, ROPE_DIM)

    merged = merge_mm(ya, yb, yc, W["w_branch"][l], gates, tm, D // 2)
    x = mm(merged, W["w_out"][l], tm, D // 2, epi="residual",
           extra=(x, g1), extra_specs=(pl.BlockSpec((tm, D // 2), lambda j, i: (i, j)),
                                       grp.vec_spec(D // 2, "ji")))
    states = (lat_out, kpe_out, shift_new, wkv_new, conv_new, ssm_new)
    return x, states, v_first


def _moe(l, h2, lw, W, tm_e):
    T, D = h2.shape
    logits = mm(h2, lw["w_router"], TM_MOE, LANES)
    gate, idx = route(logits, lw["router_bias"], TM_MOE)
    eidx = idx[:, :TOP_K]
    wsel = jnp.take_along_axis(gate, eidx, axis=1)
    flat_e = eidx.reshape(-1)
    order = jnp.argsort(flat_e, stable=True)
    sorted_e = flat_e[order]
    counts = jnp.zeros((N_EXPERTS,), jnp.int32).at[flat_e].add(1)
    starts = jnp.cumsum(counts) - counts
    pcounts = ((counts + tm_e - 1) // tm_e) * tm_e
    pends = jnp.cumsum(pcounts)
    pstarts = pends - pcounts
    rank = jnp.arange(T * TOP_K, dtype=jnp.int32)
    dest_sorted = pstarts[sorted_e] + (rank - starts[sorted_e])
    M = -(-T * TOP_K // tm_e) * tm_e + N_EXPERTS * tm_e
    src_tok = jnp.zeros((M,), jnp.int32).at[dest_sorted].set((order // TOP_K).astype(jnp.int32))
    row_w = jnp.zeros((M,), F32).at[dest_sorted].set(wsel.reshape(-1)[order])
    dest = jnp.zeros((T * TOP_K,), jnp.int32).at[order].set(dest_sorted)
    tile_start = jnp.arange(M // tm_e, dtype=jnp.int32) * tm_e
    tile_expert = jnp.minimum(jnp.searchsorted(pends, tile_start, side="right"), N_EXPERTS - 1).astype(jnp.int32)
    xs = jnp.take(h2, src_tok, axis=0)
    ys = expert_ffn(l, tile_expert, xs, W["w_exp_gate"], W["w_exp_up"], W["w_exp_down"],
                    row_w.reshape(M, 1), tm_e)
    y8 = jnp.take(ys, dest, axis=0).reshape(T, TOP_K * D)
    return combine_topk(y8, TM_MOE)


def kernel(x_prompt, x_sample, cache_mla_latent, cache_mla_rope, page_table, state_rwkv_shift, state_rwkv_wkv, state_ssm_conv, state_ssm, c_prompt, c_sample, norm_mix, norm_ffn, w_ada, b_ada, w_in, rwkv_mu, rwkv_vecs, rwkv_w_up, rwkv_a_up, rwkv_g_up, rwkv_v0, rwkv_v_down, rwkv_v_up, ssm_conv_w, ssm_conv_b, ssm_dt_bias, ssm_a_log, ssm_d, ssm_norm_w, mla_q_norm, mla_q_up, mla_kv_norm, mla_kv_up, mla_qk_gain_q, mla_qk_gain_k, w_branch, w_out, w_router, router_bias, w_exp_gate, w_exp_up, w_exp_down, w_sh_gate, w_sh_up, w_sh_down):
    W = dict(norm_mix=norm_mix, norm_ffn=norm_ffn, w_ada=w_ada, b_ada=b_ada, w_in=w_in,
             rwkv_mu=rwkv_mu, rwkv_vecs=rwkv_vecs, rwkv_w_up=rwkv_w_up, rwkv_a_up=rwkv_a_up,
             rwkv_g_up=rwkv_g_up, rwkv_v0=rwkv_v0, rwkv_v_down=rwkv_v_down, rwkv_v_up=rwkv_v_up,
             ssm_conv_w=ssm_conv_w, ssm_conv_b=ssm_conv_b, ssm_dt_bias=ssm_dt_bias,
             ssm_a_log=ssm_a_log, ssm_d=ssm_d, ssm_norm_w=ssm_norm_w, mla_q_norm=mla_q_norm,
             mla_q_up=mla_q_up, mla_kv_norm=mla_kv_norm, mla_kv_up=mla_kv_up,
             mla_qk_gain_q=mla_qk_gain_q, mla_qk_gain_k=mla_qk_gain_k, w_branch=w_branch,
             w_out=w_out, w_router=w_router, router_bias=router_bias, w_exp_gate=w_exp_gate,
             w_exp_up=w_exp_up, w_exp_down=w_exp_down, w_sh_gate=w_sh_gate, w_sh_up=w_sh_up,
             w_sh_down=w_sh_down)
    depth = w_in.shape[0]
    D = D_MODEL
    Bp, Lp = x_prompt.shape[0], x_prompt.shape[1]
    Bs, Ls = x_sample.shape[0], x_sample.shape[1]
    assert Ls == 1
    n_pages = page_table.shape[1]
    past_len = n_pages * PAGE
    gp = Group(Bp, Lp, min(TM_PROMPT, Lp))
    gs = Group(Bs, 1, Bs)
    tq = min(TQ_FLASH, Lp)

    ex = np.zeros((LANES, SSM_DIM), np.float32)
    for hh in range(HEADS):
        ex[hh, hh * HD:(hh + 1) * HD] = 1.0
    seg = np.zeros((HEADS, LANES), np.float32)
    for hh in range(HEADS):
        seg[hh, hh * 8:(hh + 1) * 8] = 1.0
    pm = np.zeros((LANES, ROPE_DIM), np.float32)
    for j in range(ROPE_DIM):
        pm[NOPE_DIM + j, j] = 1.0
    consts_p = dict(bd64=_block_diag01(MXU_DIM, HD), expand=jnp.asarray(ex, BF16),
                    cs=_rope_tables(jnp.arange(Lp, dtype=jnp.int32)))
    consts_s = dict(bd64=consts_p["bd64"], expand=consts_p["expand"],
                    cs=_rope_tables(past_len + jnp.arange(1, dtype=jnp.int32)))
    seg_b = jnp.asarray(seg, BF16)
    pm_b = jnp.asarray(pm, BF16)
    G = 16 if n_pages % 16 == 0 else 2

    def attend_prompt(l, q, k, vh, lat, kpe, lw):
        return flash_causal(Bp, Lp, q, k, vh, tq)

    def attend_paged(l, q, k, vh, lat, kpe, lw):
        ql, qr = mla_absorb(q, lw["wkv2d"], lw["gk"], pm_b)
        ql = jnp.transpose(ql, (1, 0, 2))
        qr = jnp.transpose(qr, (1, 0, 2))
        nlat8 = jnp.pad(lat[:, None, :], ((0, 0), (0, PAGE - 1), (0, 0)))
        nkpe8 = jnp.pad(kpe[:, None, NOPE_DIM:QK_DIM], ((0, 0), (0, PAGE - 1), (0, 0)))
        acc = mla_decode(l, cache_mla_latent, cache_mla_rope, page_table, ql, qr, lw["wk_perm"], seg_b,
                         nlat8, nkpe8, G)
        return mla_vup(jnp.transpose(acc, (1, 0, 2)), lw["wkv2d"])

    xp = x_prompt.reshape(Bp * Lp, D)
    xsm = x_sample.reshape(Bs, D)
    cp8 = jnp.pad(c_prompt, ((0, SUBLANES - Bp % SUBLANES if Bp % SUBLANES else 0), (0, 0)))
    outs_p = [[] for _ in range(6)]
    outs_s = [[] for _ in range(6)]
    vf_p = vf_s = None
    tm_e = TM_EXPERT
    for l in range(depth):
        lw = _layer_weights(l, W)
        bada = b_ada[l].reshape(1, 6 * D)
        bspec = pl.BlockSpec((1, D // 2), lambda j, i: (0, j))
        mod_p = mm(cp8, w_ada[l], cp8.shape[0], D // 2, act_in="silu", epi="bias",
                   extra=(bada,), extra_specs=(bspec,))[:Bp]
        mod_s = mm(c_sample, w_ada[l], Bs, D // 2, act_in="silu", epi="bias",
                   extra=(bada,), extra_specs=(bspec,))
        mods_p = [gp.vec(mod_p[:, i * D:(i + 1) * D]) for i in range(6)]
        mods_s = [gs.vec(mod_s[:, i * D:(i + 1) * D]) for i in range(6)]
        st_s = (state_rwkv_shift[l], state_rwkv_wkv[l], state_ssm_conv[l], state_ssm[l])
        xp, st_p_new, vf_p = _mix_block(gp, l, xp, mods_p, (None, None, None, None), lw, W, consts_p, vf_p,
                                        attend_prompt)
        xsm, st_s_new, vf_s = _mix_block(gs, l, xsm, mods_s, st_s, lw, W, consts_s, vf_s, attend_paged)
        for lst, arr in zip(outs_p, st_p_new):
            lst.append(arr)
        for lst, arr in zip(outs_s, st_s_new):
            lst.append(arr)
        h2p = norm_mod(gp, xp, norm_ffn[l], mods_p[4], mods_p[3])
        h2s = norm_mod(gs, xsm, norm_ffn[l], mods_s[4], mods_s[3])
        h2 = jnp.concatenate([h2p, h2s], axis=0)
        routed = _moe(l, h2, lw, W, tm_e)
        hid = glu_mm(h2, w_sh_gate[l], w_sh_up[l], TM_MOE)
        Tp = Bp * Lp
        half = D // 2
        xp = mm(hid[:Tp], w_sh_down[l], gp.tm, half, epi="residual2",
                extra=(xp, mods_p[5], routed[:Tp]),
                extra_specs=(pl.BlockSpec((gp.tm, half), lambda j, i: (i, j)), gp.vec_spec(half, "ji"),
                             pl.BlockSpec((gp.tm, half), lambda j, i: (i, j))))
        xsm = mm(hid[Tp:], w_sh_down[l], gs.tm, half, epi="residual2",
                 extra=(xsm, mods_s[5], routed[Tp:]),
                 extra_specs=(pl.BlockSpec((gs.tm, half), lambda j, i: (i, j)), gs.vec_spec(half, "ji"),
                              pl.BlockSpec((gs.tm, half), lambda j, i: (i, j))))
    y_prompt = xp.reshape(Bp, Lp, D)
    y_sample = xsm.reshape(Bs, 1, D)
    sp = [jnp.stack(o) for o in outs_p]
    ss = [jnp.stack(o) for o in outs_s]
    ss[0] = ss[0].reshape(depth, Bs, 1, KV_LORA)
    ss[1] = ss[1].reshape(depth, Bs, 1, ROPE_DIM)
    return (y_prompt, y_sample, sp[0], sp[1], sp[2], sp[3], sp[4], sp[5],
            ss[0], ss[1], ss[2], ss[3], ss[4], ss[5])
```

```python
import functools
import math

import jax
import jax.numpy as jnp
import numpy as np
from jax import lax
from jax.experimental import pallas as pl
from jax.experimental.pallas import tpu as pltpu

F32 = jnp.float32
BF16 = jnp.bfloat16

LANES = 128
SUBLANES = 8
MXU_DIM = 256
VMEM_LIMIT = 56 * 1024 * 1024

D_MODEL = 2048
HEADS = 16
HD = 64
RWKV_DIM = HEADS * HD
W_LORA, A_LORA, V_LORA, G_LORA = 64, 64, 32, 128
RWKV_IN = 3 * RWKV_DIM + W_LORA + A_LORA + G_LORA
RWKV_GN_EPS = 64e-5
SSM_DIM = HEADS * HD
SSM_STATE = 128
SSM_GROUPS = 2
CONV_W = 4
CONV_DIM = SSM_DIM + 2 * SSM_GROUPS * SSM_STATE
SSM_IN = SSM_DIM + CONV_DIM + HEADS
SSD_CHUNK = 128
NOPE_DIM, ROPE_DIM = 64, 32
QK_DIM = NOPE_DIM + ROPE_DIM
Q_LORA, KV_LORA = 512, 256
MLA_IN = Q_LORA + KV_LORA + ROPE_DIM
ROPE_THETA = 10000.0
ATTN_SCALE = QK_DIM ** -0.5
N_BRANCH = 3
N_EXPERTS = 64
TOP_K = 8
N_EXPERT_GROUPS = 8
TOPK_GROUPS = 4
EXPERT_FF = 512
ROUTED_SCALE = 2.5
NORM_EPS = 1e-6
NEG_INF = -1e30
PAGE = 128

TM_PROMPT = 512
TQ_FLASH = 512
TM_MOE = 384
TM_EXPERT = 512
SCAN_TB = 32
DECODE_PAGES = 32


def _cparams(sem):
    return pltpu.CompilerParams(dimension_semantics=sem, vmem_limit_bytes=VMEM_LIMIT)


def _dot(a, b):
    return jnp.dot(a, b, preferred_element_type=F32)


def _dot_nt(a, b):
    return lax.dot_general(a, b, (((1,), (1,)), ((), ())), preferred_element_type=F32)


def _split2(x):
    hi = x.astype(BF16)
    lo = (x - hi.astype(F32)).astype(BF16)
    return hi, lo


def _split3(x):
    hi = x.astype(BF16)
    r = x - hi.astype(F32)
    mid = r.astype(BF16)
    lo = (r - mid.astype(F32)).astype(BF16)
    return hi, mid, lo


def _dot01(x, m01):
    a, b, c = _split3(x)
    return _dot(a, m01) + _dot(b, m01) + _dot(c, m01)


def _segsum(x, bd):
    hi, lo = _split2(x)
    outs = []
    for c in range(x.shape[1] // MXU_DIM):
        sl = slice(c * MXU_DIM, (c + 1) * MXU_DIM)
        outs.append(_dot(hi[:, sl], bd) + _dot(lo[:, sl], bd))
    return outs[0] if len(outs) == 1 else jnp.concatenate(outs, axis=1)


def _softplus(x):
    return jnp.maximum(x, 0.0) + jnp.log1p(jnp.exp(-jnp.abs(x)))


def _sigmoid(x):
    return 1.0 / (1.0 + jnp.exp(-x))


def _silu(x):
    return x * _sigmoid(x)


def _block_diag01(n, seg):
    i = np.arange(n)
    return jnp.asarray((i[:, None] // seg) == (i[None, :] // seg), dtype=BF16)


class Group:
    def __init__(self, batch, seqlen, tm):
        self.B, self.L, self.T, self.tm = batch, seqlen, batch * seqlen, tm
        self.per_token = seqlen == 1
        assert self.T % tm == 0
        assert self.per_token or seqlen % tm == 0
        self.ntiles = self.T // tm

    def vec(self, a):
        return a[None] if self.per_token else a[:, None, :]

    def vec_spec(self, tn, order):
        tm, L = self.tm, self.L
        if self.per_token:
            if order == "ji":
                return pl.BlockSpec((None, tm, tn), lambda j, i: (0, i, j))
            return pl.BlockSpec((None, tm, tn), lambda i: (0, i, 0))
        if order == "ji":
            return pl.BlockSpec((None, 1, tn), lambda j, i: ((i * tm) // L, 0, j))
        return pl.BlockSpec((None, 1, tn), lambda i: ((i * tm) // L, 0, 0))


def _mm_body(*refs, act_in, epi, n_extra):
    a_ref, w_ref = refs[0], refs[1]
    extra = refs[2:2 + n_extra]
    o_ref = refs[2 + n_extra]
    wb_ref = refs[3 + n_extra]

    @pl.when(pl.program_id(1) == 0)
    def _():
        wb_ref[...] = w_ref[...].astype(BF16)

    a = a_ref[...]
    if act_in == "silu":
        a = _silu(a.astype(F32))
    acc = _dot(a.astype(BF16), wb_ref[...])
    if epi == "bias":
        acc = acc + extra[0][...]
    elif epi == "sigmoid":
        acc = _sigmoid(acc)
    elif epi == "residual":
        acc = extra[0][...] + extra[1][...] * acc
    elif epi == "residual_moe":
        routed = extra[2][0].astype(F32)
        for k in range(1, TOP_K):
            routed = routed + extra[2][k].astype(F32)
        acc = extra[0][...] + extra[1][...] * (acc + routed)
    o_ref[...] = acc.astype(o_ref.dtype)


def mm(a, w, tm, tn, *, layer=None, rows=None, row0=0, act_in=None, epi=None, extra=(), extra_specs=(),
       out_dtype=F32):
    K = a.shape[1]
    M = a.shape[0] if rows is None else rows
    N = w.shape[-1]
    assert M % tm == 0 and N % tn == 0 and row0 % tm == 0, (M, tm, N, tn, row0)
    i0 = row0 // tm
    if layer is None:
        w_spec = pl.BlockSpec((K, tn), lambda j, i: (0, j))
    else:
        w_spec = pl.BlockSpec((None, K, tn), lambda j, i: (layer, 0, j))
    body = functools.partial(_mm_body, act_in=act_in, epi=epi, n_extra=len(extra))
    return pl.pallas_call(
        body,
        grid=(N // tn, M // tm),
        in_specs=[pl.BlockSpec((tm, K), lambda j, i: (i0 + i, 0)), w_spec] + list(extra_specs),
        out_specs=pl.BlockSpec((tm, tn), lambda j, i: (i, j)),
        out_shape=jax.ShapeDtypeStruct((M, N), out_dtype),
        scratch_shapes=[pltpu.VMEM((K, tn), BF16)],
        compiler_params=_cparams(("arbitrary", "arbitrary")),
    )(a, w, *extra)


def _norm_mod_body(x_ref, g_ref, sc_ref, sh_ref, o_ref):
    x = x_ref[...]
    y = x * lax.rsqrt(jnp.mean(x * x, axis=-1, keepdims=True) + NORM_EPS) * g_ref[...]
    o_ref[...] = (y * (1.0 + sc_ref[...]) + sh_ref[...]).astype(o_ref.dtype)


def norm_mod(grp, x, g, sc, sh):
    T, D = x.shape
    tm = grp.tm
    return pl.pallas_call(
        _norm_mod_body,
        grid=(T // tm,),
        in_specs=[pl.BlockSpec((tm, D), lambda i: (i, 0)),
                  pl.BlockSpec((1, D), lambda i: (0, 0)),
                  grp.vec_spec(D, "i"), grp.vec_spec(D, "i")],
        out_specs=pl.BlockSpec((tm, D), lambda i: (i, 0)),
        out_shape=jax.ShapeDtypeStruct((T, D), BF16),
        compiler_params=_cparams(("arbitrary",)),
    )(x, g.reshape(1, D), sc, sh)


def _rwkv_prep_body(*refs, layer1, per_token, tiles_per_seq):
    z_ref, prev_ref, mu_ref, vec_ref, wup_ref, aup_ref, gup_ref, bd_ref = refs[:8]
    rest = refs[8:]
    if layer1:
        v0_ref, vdn_ref, vup_ref, vfirst_ref = rest[:4]
        rest = rest[4:]
    r_o, w_o, k_o, v_o, kk_o, b_o, g_o, carry = rest
    z = z_ref[...]
    tm = z.shape[0]
    if per_token:
        prev = prev_ref[...]
    else:
        first = (pl.program_id(0) % tiles_per_seq) == 0
        prow = jnp.where(first, prev_ref[...], carry[...])
        rid = lax.broadcasted_iota(jnp.int32, z.shape, 0)
        prev = jnp.where(rid == 0, prow, pltpu.roll(z, 1, 0))
        carry[...] = z[tm - 1:tm, :]
    zs = z + mu_ref[...] * (prev - z)
    D = RWKV_DIM
    r, k, v = zs[:, 0:D], zs[:, D:2 * D], zs[:, 2 * D:3 * D]
    lora = zs[:, 3 * D:3 * D + LANES]
    xg = zs[:, 3 * D + LANES:3 * D + 2 * LANES]
    w0, a0 = vec_ref[0:1, :], vec_ref[1:2, :]
    k_k, k_a = vec_ref[2:3, :], vec_ref[3:4, :]
    logw = -_softplus(-(w0 + _dot(jnp.tanh(lora).astype(BF16), wup_ref[...]))) - 0.5
    decay = jnp.exp(-jnp.exp(logw))
    a = _sigmoid(a0 + _dot(lora.astype(BF16), aup_ref[...]))
    g = _dot(_sigmoid(xg).astype(BF16), gup_ref[...])
    if layer1:
        lo = _dot(v.astype(BF16), vdn_ref[...])
        gate_v = _sigmoid(v0_ref[...] + _dot(lo.astype(BF16), vup_ref[...]))
        v = v + (vfirst_ref[...] - v) * gate_v
    kk = k * k_k
    ss = _segsum(kk * kk, bd_ref[...])
    kk = kk / jnp.maximum(jnp.sqrt(ss), 1e-12)
    r_o[...] = r
    w_o[...] = decay
    k_o[...] = k * (1.0 + (a - 1.0) * k_a)
    v_o[...] = v
    kk_o[...] = kk
    b_o[...] = kk * a
    g_o[...] = g


def rwkv_prep(grp, zA, prev, mu, vecs8, wup, aup, gup, bd, layer1_args):
    T = zA.shape[0]
    tm = grp.tm
    D = RWKV_DIM
    layer1 = layer1_args is not None
    full = lambda shp: pl.BlockSpec(shp, lambda i: tuple(0 for _ in shp))
    tile = lambda n: pl.BlockSpec((tm, n), lambda i: (i, 0))
    if grp.per_token:
        prev_spec = tile(RWKV_IN)
        tiles_per_seq = 1
    else:
        tiles_per_seq = grp.L // tm
        prev_spec = pl.BlockSpec((None, 1, RWKV_IN), lambda i: (i // tiles_per_seq, 0, 0))
    in_specs = [tile(RWKV_IN), prev_spec, full((1, RWKV_IN)), full((8, D)),
                full((LANES, D)), full((LANES, D)), full((LANES, D)), full((MXU_DIM, MXU_DIM))]
    args = [zA, prev, mu, vecs8, wup, aup, gup, bd]
    if layer1:
        v0, vdn, vup, vfirst = layer1_args
        in_specs += [full((1, D)), full((D, LANES)), full((LANES, D)), tile(D)]
        args += [v0, vdn, vup, vfirst]
    body = functools.partial(_rwkv_prep_body, layer1=layer1, per_token=grp.per_token,
                             tiles_per_seq=tiles_per_seq)
    return pl.pallas_call(
        body,
        grid=(T // tm,),
        in_specs=in_specs,
        out_specs=[tile(D)] * 7,
        out_shape=[jax.ShapeDtypeStruct((T, D), F32)] * 7,
        scratch_shapes=[pltpu.VMEM((1, RWKV_IN), F32)],
        compiler_params=_cparams(("arbitrary",)),
    )(*args)


def _rwkv_scan_body(w_ref, kk_ref, b_ref, k_ref, r_ref, v_ref, s0_ref, y_ref, sT_ref, S, *dup, TB, NV, ns):
    tb = pl.program_id(1)

    @pl.when(tb == 0)
    def _():
        S[...] = s0_ref[...]

    srcs = (w_ref, kk_ref, b_ref, k_ref, r_ref)
    if ns > 1:
        for src, dst in zip(srcs, dup):
            x = src[...]
            dst[...] = jnp.concatenate([x] * ns, axis=-1)
        srcs = dup

    def step(t, carry):
        w, kk, bb, k, r = (ref[t] for ref in srcs)
        for vi in range(NV):
            s = S[vi]
            sa = jnp.sum(s * kk, axis=0, keepdims=True)
            vv = v_ref[t, pl.ds(vi, 1), :]
            s = s * w - sa * bb + vv * k
            S[vi] = s
            y_ref[t, pl.ds(vi, 1), :] = jnp.sum(s * r, axis=0, keepdims=True)
        return carry

    lax.fori_loop(0, TB, step, 0)

    @pl.when(tb == pl.num_programs(1) - 1)
    def _():
        sT_ref[...] = S[...]


def rwkv_scan(w, kk, b, k, r, v, s0, TB):
    T, NV, LN = v.shape
    ns = HD // NV
    assert T % TB == 0 and LN % LANES == 0 and w.shape[2] * ns == LN
    assert ns == 1 or LN == LANES
    kspec = pl.BlockSpec((TB, HD, LANES // ns), lambda c, t: (t, 0, c))
    vspec = pl.BlockSpec((TB, NV, LANES), lambda c, t: (t, 0, c))
    sspec = pl.BlockSpec((NV, HD, LANES), lambda c, t: (0, 0, c))
    return pl.pallas_call(
        functools.partial(_rwkv_scan_body, TB=TB, NV=NV, ns=ns),
        grid=(LN // LANES, T // TB),
        in_specs=[kspec] * 5 + [vspec, sspec],
        out_specs=[vspec, sspec],
        out_shape=[jax.ShapeDtypeStruct((T, NV, LN), F32), jax.ShapeDtypeStruct((NV, HD, LN), F32)],
        scratch_shapes=[pltpu.VMEM((NV, HD, LANES), F32)]
        + ([pltpu.VMEM((TB, HD, LANES), F32)] * 5 if ns > 1 else []),
        compiler_params=_cparams(("arbitrary", "arbitrary")),
    )(w, kk, b, k, r, v, s0)


def _rwkv_post_body(y_ref, r_ref, k_ref, v_ref, g_ref, vec_ref, bd_ref, o_ref):
    bd = bd_ref[...]
    y = y_ref[...]
    r_k, ln_w, ln_b = vec_ref[4:5, :], vec_ref[5:6, :], vec_ref[6:7, :]
    mean = _segsum(y, bd) * (1.0 / HD)
    d = y - mean
    var = _segsum(d * d, bd) * (1.0 / HD)
    yn = d * lax.rsqrt(var + RWKV_GN_EPS) * ln_w + ln_b
    v = v_ref[...]
    bonus = _segsum(r_ref[...] * k_ref[...] * r_k, bd) * v
    o_ref[...] = ((yn + bonus) * g_ref[...]).astype(o_ref.dtype)


def rwkv_post(grp, y, r, kmod, v, g, vecs8, bd):
    T, D = y.shape
    tm = grp.tm
    tile = pl.BlockSpec((tm, D), lambda i: (i, 0))
    return pl.pallas_call(
        _rwkv_post_body,
        grid=(T // tm,),
        in_specs=[tile] * 5 + [pl.BlockSpec((8, D), lambda i: (0, 0)),
                               pl.BlockSpec((MXU_DIM, MXU_DIM), lambda i: (0, 0))],
        out_specs=tile,
        out_shape=jax.ShapeDtypeStruct((T, D), BF16),
        compiler_params=_cparams(("arbitrary",)),
    )(y, r, kmod, v, g, vecs8, bd)


def _ssd_pre_body(*refs, per_token, tiles_per_seq):
    if per_token:
        x_ref, c0_ref, c1_ref, c2_ref, cw_ref, cb_ref, sm_ref, dtb_ref, xs_o, bc_o, dt_o = refs
        x = x_ref[...]
        taps = [c0_ref[...], c1_ref[...], c2_ref[...], x]
    else:
        x_ref, cw_ref, cb_ref, sm_ref, dtb_ref, xs_o, bc_o, dt_o, carry = refs
        x = x_ref[...]
        tm = x.shape[0]

        @pl.when((pl.program_id(0) % tiles_per_seq) == 0)
        def _():
            carry[...] = jnp.zeros_like(carry)

        ext = jnp.concatenate([carry[...], x], axis=0)
        taps = [pltpu.roll(ext, CONV_W - 1 - i, 0)[SUBLANES:, :] for i in range(CONV_W - 1)] + [x]
        carry[...] = x[tm - SUBLANES:, :]
    conv = cb_ref[...] + taps[0] * cw_ref[0:1, :]
    for i in range(1, CONV_W):
        conv = conv + taps[i] * cw_ref[i:i + 1, :]
    xc = _silu(conv)
    xs_o[...] = xc[:, :SSM_DIM]
    bc_o[...] = xc[:, SSM_DIM:]
    dt_o[...] = _softplus(sm_ref[...] + dtb_ref[...])


def ssd_pre(grp, xbc, conv0, cw8, cb, zsm_dt, dtb):
    T = xbc.shape[0]
    tm = grp.tm
    tile = lambda n: pl.BlockSpec((tm, n), lambda i: (i, 0))
    full = lambda shp: pl.BlockSpec(shp, lambda i: tuple(0 for _ in shp))
    w_specs = [full((8, CONV_DIM)), full((1, CONV_DIM)), tile(LANES), full((1, LANES))]
    if grp.per_token:
        in_specs = [tile(CONV_DIM)] * 4 + w_specs
        args = [xbc, conv0[:, 0], conv0[:, 1], conv0[:, 2], cw8, cb, zsm_dt, dtb]
        scratch = []
        tiles_per_seq = 1
    else:
        in_specs = [tile(CONV_DIM)] + w_specs
        args = [xbc, cw8, cb, zsm_dt, dtb]
        scratch = [pltpu.VMEM((SUBLANES, CONV_DIM), F32)]
        tiles_per_seq = grp.L // tm
    return pl.pallas_call(
        functools.partial(_ssd_pre_body, per_token=grp.per_token, tiles_per_seq=tiles_per_seq),
        grid=(T // tm,),
        in_specs=in_specs,
        out_specs=[tile(SSM_DIM), tile(CONV_DIM - SSM_DIM), tile(LANES)],
        out_shape=[jax.ShapeDtypeStruct((T, SSM_DIM), F32),
                   jax.ShapeDtypeStruct((T, CONV_DIM - SSM_DIM), F32),
                   jax.ShapeDtypeStruct((T, LANES), F32)],
        scratch_shapes=scratch,
        compiler_params=_cparams(("arbitrary",)),
    )(*args)


def _ssd_chunk_body(xs_ref, bc_ref, dtc_ref, dtr_ref, ac_ref, ar_ref, ex_ref, y_ref, sT_ref, St):
    Q = SSD_CHUNK
    c = pl.program_id(1)

    @pl.when(c == 0)
    def _():
        St[...] = jnp.zeros_like(St)

    ri = lax.broadcasted_iota(jnp.int32, (Q, Q), 0)
    ci = lax.broadcasted_iota(jnp.int32, (Q, Q), 1)
    causal = ri >= ci
    tri = jnp.where(causal, 1.0, 0.0).astype(BF16)
    triT = jnp.where(ri <= ci, 1.0, 0.0).astype(BF16)
    dtc = dtc_ref[...]
    a_col = dtc * (-jnp.exp(ac_ref[...]))
    a_row = dtr_ref[...] * (-jnp.exp(ar_ref[...]))
    acs_col = _dot01_lhs(tri, a_col)
    acs_row = _dot01(a_row, triT)
    ex = ex_ref[...]
    dt_full = _dot01(dtc, ex)
    acs_full = _dot01(acs_col, ex)
    xs = xs_ref[...]
    xdt = xs * dt_full
    xdec = xdt * jnp.exp(acs_full[Q - 1:Q, :] - acs_full)
    eacs = jnp.exp(acs_full)
    lane = lax.broadcasted_iota(jnp.int32, (Q, LANES), 1)
    row = lax.broadcasted_iota(jnp.int32, (LANES, LANES), 0)
    xdt_b = xdt.astype(BF16)
    for g in range(SSM_GROUPS):
        bm = bc_ref[:, g * SSM_STATE:(g + 1) * SSM_STATE].astype(BF16)
        cm = bc_ref[:, (SSM_GROUPS + g) * SSM_STATE:(SSM_GROUPS + g + 1) * SSM_STATE].astype(BF16)
        cb = _dot_nt(cm, bm)
        for jp in range(HEADS // SSM_GROUPS // 2):
            j = g * (HEADS // SSM_GROUPS // 2) + jp
            sl = slice(j * LANES, (j + 1) * LANES)
            outs = []
            for hh in range(2):
                h = 2 * j + hh
                diff = acs_col[:, h:h + 1] - acs_row[h:h + 1, :]
                lm = jnp.where(causal, jnp.exp(jnp.minimum(diff, 0.0)), 0.0)
                outs.append(_dot((cb * lm).astype(BF16), xdt_b[:, sl]))
            y_diag = jnp.where(lane < HD, outs[0], outs[1])
            s_old = St[j]
            y_off = _dot_nt(cm, s_old.astype(BF16)) * eacs[:, sl]
            y_ref[:, sl] = y_diag + y_off
            new = _dot(jnp.transpose(xdec[:, sl]).astype(BF16), bm)
            tot0 = jnp.exp(acs_row[2 * j:2 * j + 1, Q - 1:Q])
            tot1 = jnp.exp(acs_row[2 * j + 1:2 * j + 2, Q - 1:Q])
            St[j] = s_old * jnp.where(row < HD, tot0, tot1) + new

    @pl.when(c == pl.num_programs(1) - 1)
    def _():
        sT_ref[...] = St[...]


def _dot01_lhs(m01, x):
    a, b, c = _split3(x)
    return _dot(m01, a) + _dot(m01, b) + _dot(m01, c)


def ssd_chunk(B, L, xs, bc, dtp, dtpT, alog_col, alog_row, expand):
    Q = SSD_CHUNK
    nc = L // Q
    return pl.pallas_call(
        _ssd_chunk_body,
        grid=(B, nc),
        in_specs=[pl.BlockSpec((Q, SSM_DIM), lambda b, c: (b * nc + c, 0)),
                  pl.BlockSpec((Q, CONV_DIM - SSM_DIM), lambda b, c: (b * nc + c, 0)),
                  pl.BlockSpec((Q, LANES), lambda b, c: (b * nc + c, 0)),
                  pl.BlockSpec((None, LANES, Q), lambda b, c: (b, 0, c)),
                  pl.BlockSpec((1, LANES), lambda b, c: (0, 0)),
                  pl.BlockSpec((LANES, 1), lambda b, c: (0, 0)),
                  pl.BlockSpec((LANES, SSM_DIM), lambda b, c: (0, 0))],
        out_specs=[pl.BlockSpec((Q, SSM_DIM), lambda b, c: (b * nc + c, 0)),
                   pl.BlockSpec((None, HEADS // 2, LANES, SSM_STATE), lambda b, c: (b, 0, 0, 0))],
        out_shape=[jax.ShapeDtypeStruct((B * L, SSM_DIM), F32),
                   jax.ShapeDtypeStruct((B, HEADS // 2, LANES, SSM_STATE), F32)],
        scratch_shapes=[pltpu.VMEM((HEADS // 2, LANES, SSM_STATE), F32)],
        compiler_params=_cparams(("arbitrary", "arbitrary")),
    )(xs, bc, dtp, dtpT, alog_row, alog_col, expand)


def _ssd_step_body(h_ref, x_ref, b_ref, c_ref, dt_ref, al_ref, y_ref, hn_ref):
    dt = dt_ref[...]
    dA = jnp.exp(dt * (-jnp.exp(al_ref[...])))
    bm, cm = b_ref[...], c_ref[...]
    for p in range(HD):
        hp = h_ref[p] * dA + (x_ref[p:p + 1, :] * dt) * bm
        hn_ref[p] = hp
        y_ref[p:p + 1, :] = jnp.sum(hp * cm, axis=0, keepdims=True)


def ssd_step(h0, x, bm, cm, dt, alog):
    LN = h0.shape[-1]
    lt = lambda n: pl.BlockSpec((n, LANES), lambda c: (0, c))
    hs = pl.BlockSpec((HD, SSM_STATE, LANES), lambda c: (0, 0, c))
    return pl.pallas_call(
        _ssd_step_body,
        grid=(LN // LANES,),
        in_specs=[hs, lt(HD), lt(SSM_STATE), lt(SSM_STATE), lt(1), lt(1)],
        out_specs=[lt(HD), hs],
        out_shape=[jax.ShapeDtypeStruct((HD, LN), F32), jax.ShapeDtypeStruct((HD, SSM_STATE, LN), F32)],
        compiler_params=_cparams(("arbitrary",)),
    )(h0, x, bm, cm, dt, alog)


def _ssd_post_body(y_ref, xs_ref, zg_ref, d_ref, nw_ref, o_ref):
    y = (y_ref[...] + d_ref[...] * xs_ref[...]) * _silu(zg_ref[...])
    gs = SSM_DIM // SSM_GROUPS
    for g in range(SSM_GROUPS):
        sl = slice(g * gs, (g + 1) * gs)
        yg = y[:, sl]
        n = yg * lax.rsqrt(jnp.mean(yg * yg, axis=-1, keepdims=True) + NORM_EPS)
        o_ref[:, sl] = (n * nw_ref[:, sl]).astype(o_ref.dtype)


def ssd_post(grp, y, xs, zg, dfull, nw):
    T, D = y.shape
    tm = grp.tm
    tile = pl.BlockSpec((tm, D), lambda i: (i, 0))
    row = pl.BlockSpec((1, D), lambda i: (0, 0))
    return pl.pallas_call(
        _ssd_post_body,
        grid=(T // tm,),
        in_specs=[tile, tile, tile, row, row],
        out_specs=tile,
        out_shape=jax.ShapeDtypeStruct((T, D), BF16),
        compiler_params=_cparams(("arbitrary",)),
    )(y, xs, zg, dfull, nw)


def _mla_prep_body(zq_ref, zsm_ref, cs_ref, qn_ref, kvn_ref, wq_ref, wkv_ref, gq_ref, gk_ref,
                   q_o, k_o, v_o, lat_o, kpe_o, wq_b, wkv_b):
    @pl.when(pl.program_id(0) == 0)
    def _():
        wq_b[...] = wq_ref[...].astype(BF16)
        wkv_b[...] = wkv_ref[...].astype(BF16)

    HP = HEADS * LANES
    zq = zq_ref[...]
    qd, kvd = zq[:, :Q_LORA], zq[:, Q_LORA:]
    qn = qd * lax.rsqrt(jnp.mean(qd * qd, axis=-1, keepdims=True) + NORM_EPS) * qn_ref[...]
    lat = kvd * lax.rsqrt(jnp.mean(kvd * kvd, axis=-1, keepdims=True) + NORM_EPS) * kvn_ref[...]
    lat_o[...] = lat
    cosv, sinv = cs_ref[:, :LANES], cs_ref[:, LANES:]
    kpe = zsm_ref[:, :LANES] * cosv + zsm_ref[:, LANES:2 * LANES] * sinv
    kpe_o[...] = kpe
    qq = _dot(qn.astype(BF16), wq_b[...])
    kv = _dot(lat.astype(BF16), wkv_b[...])
    v_o[...] = kv[:, HP:].astype(v_o.dtype)
    gq, gk = gq_ref[...], gk_ref[...]
    nope_one = jnp.where(lax.broadcasted_iota(jnp.int32, (1, LANES), 1) < NOPE_DIM, 1.0, 0.0)
    cq = cosv + nope_one
    for h in range(HEADS):
        sl = slice(h * LANES, (h + 1) * LANES)
        qh = qq[:, sl] * cq + qq[:, HP + h * LANES:HP + (h + 1) * LANES] * sinv
        rq = lax.rsqrt(jnp.sum(qh * qh, axis=-1, keepdims=True) * (1.0 / QK_DIM) + NORM_EPS)
        q_o[:, sl] = (qh * rq * gq).astype(q_o.dtype)
        kh = kv[:, sl] + kpe
        rk = lax.rsqrt(jnp.sum(kh * kh, axis=-1, keepdims=True) * (1.0 / QK_DIM) + NORM_EPS)
        k_o[:, sl] = (kh * rk * gk).astype(k_o.dtype)


def mla_prep(grp, zq, zsm_kpe, cs, qn, kvn, wq2, wkv, gq, gk):
    T = zq.shape[0]
    tm = grp.tm
    HP = HEADS * LANES
    tile = lambda n: pl.BlockSpec((tm, n), lambda i: (i, 0))
    full = lambda shp: pl.BlockSpec(shp, lambda i: tuple(0 for _ in shp))
    if grp.per_token:
        cs_spec = full((1, 2 * LANES))
    else:
        tps = grp.L // tm
        cs_spec = pl.BlockSpec((tm, 2 * LANES), lambda i: (i % tps, 0))
    return pl.pallas_call(
        _mla_prep_body,
        grid=(T // tm,),
        in_specs=[tile(Q_LORA + KV_LORA), tile(2 * LANES), cs_spec, full((1, Q_LORA)), full((1, KV_LORA)),
                  full((Q_LORA, 2 * HP)), full((KV_LORA, HP + HEADS * HD)), full((1, LANES)), full((1, LANES))],
        out_specs=[tile(HP), tile(HP), tile(HEADS * HD), tile(KV_LORA), tile(LANES)],
        out_shape=[jax.ShapeDtypeStruct((T, HP), BF16), jax.ShapeDtypeStruct((T, HP), BF16),
                   jax.ShapeDtypeStruct((T, HEADS * HD), BF16), jax.ShapeDtypeStruct((T, KV_LORA), F32),
                   jax.ShapeDtypeStruct((T, LANES), F32)],
        scratch_shapes=[pltpu.VMEM((Q_LORA, 2 * HP), BF16), pltpu.VMEM((KV_LORA, HP + HEADS * HD), BF16)],
        compiler_params=_cparams(("arbitrary",)),
    )(zq, zsm_kpe, cs, qn, kvn, wq2, wkv, gq, gk)


def _flash_body(q_ref, k_ref, v_ref, o_ref, *, tq):
    qi = pl.program_id(2)
    lane = lax.broadcasted_iota(jnp.int32, (tq, LANES), 1)
    ri = lax.broadcasted_iota(jnp.int32, (tq, tq), 0)
    ci = lax.broadcasted_iota(jnp.int32, (tq, tq), 1)
    qs = [q_ref[:, hh * LANES:(hh + 1) * LANES] for hh in range(2)]

    def block(ki, carry, masked):
        off = pl.multiple_of(ki * tq, tq)
        kb = k_ref[pl.ds(off, tq), :]
        vb = v_ref[pl.ds(off, tq), :]
        out = []
        for hh in range(2):
            m, l, acc = carry[hh]
            s = _dot_nt(qs[hh], kb[:, hh * LANES:(hh + 1) * LANES])
            if masked:
                s = jnp.where(ci <= ri, s, NEG_INF)
            m_new = jnp.maximum(m, jnp.max(s, axis=-1, keepdims=True))
            corr = jnp.exp(m - m_new)
            p = jnp.exp(s - m_new)
            l = l * corr + jnp.sum(p, axis=-1, keepdims=True)
            acc = acc * corr + _dot(p.astype(BF16), vb)
            out.append((m_new, l, acc))
        return tuple(out)

    init = tuple((jnp.full((tq, 1), NEG_INF, F32), jnp.zeros((tq, 1), F32), jnp.zeros((tq, LANES), F32))
                 for _ in range(2))
    carry = lax.fori_loop(0, qi, lambda ki, c: block(ki, c, False), init)
    carry = block(qi, carry, True)
    o0 = carry[0][2] / carry[0][1]
    o1 = carry[1][2] / carry[1][1]
    o_ref[...] = jnp.where(lane < HD, o0, o1).astype(o_ref.dtype)


def flash_causal(B, L, q, k, v, tq):
    nq = L // tq
    PW = 2 * LANES
    return pl.pallas_call(
        functools.partial(_flash_body, tq=tq),
        grid=(B, HEADS // 2, nq),
        in_specs=[pl.BlockSpec((tq, PW), lambda b, hp, i: (b * nq + i, hp)),
                  pl.BlockSpec((L, PW), lambda b, hp, i: (b, hp)),
                  pl.BlockSpec((L, LANES), lambda b, hp, i: (b, hp))],
        out_specs=pl.BlockSpec((tq, LANES), lambda b, hp, i: (b * nq + i, hp)),
        out_shape=jax.ShapeDtypeStruct((B * L, HEADS * HD), BF16),
        compiler_params=_cparams(("arbitrary", "arbitrary", "arbitrary")),
    )(q, k, v)


def _absorb_body(q_ref, wkv_ref, gk_ref, pm_ref, ql_o, qr_o):
    q = q_ref[...].astype(F32) * gk_ref[...]
    lane = lax.broadcasted_iota(jnp.int32, q.shape, 1)
    qn = jnp.where(lane < NOPE_DIM, q, 0.0).astype(BF16)
    ql_o[...] = _dot_nt(qn, wkv_ref[...].astype(BF16)).astype(ql_o.dtype)
    qr_o[...] = _dot(q.astype(BF16), pm_ref[...]).astype(qr_o.dtype)


def mla_absorb(qpad, wkv2d, gk, pm):
    Bd = qpad.shape[0]
    return pl.pallas_call(
        _absorb_body,
        grid=(HEADS,),
        in_specs=[pl.BlockSpec((Bd, LANES), lambda h: (0, h)),
                  pl.BlockSpec((KV_LORA, LANES), lambda h: (0, h)),
                  pl.BlockSpec((1, LANES), lambda h: (0, 0)),
                  pl.BlockSpec((LANES, ROPE_DIM), lambda h: (0, 0))],
        out_specs=[pl.BlockSpec((None, Bd, KV_LORA), lambda h: (h, 0, 0)),
                   pl.BlockSpec((None, Bd, ROPE_DIM), lambda h: (h, 0, 0))],
        out_shape=[jax.ShapeDtypeStruct((HEADS, Bd, KV_LORA), BF16),
                   jax.ShapeDtypeStruct((HEADS, Bd, ROPE_DIM), BF16)],
        compiler_params=_cparams(("arbitrary",)),
    )(qpad, wkv2d, gk, pm)


def _decode_body(pt_ref, *refs, G):
    lat_refs = refs[:G]
    kpe_refs = refs[G:2 * G]
    ql_ref, qr_ref, wkt_ref, nlat_ref, nkpe_ref, o_ref, lhs, s_s, lat_s, m_s, l_s, acc_s = refs[2 * G:]
    g = pl.program_id(1)
    NK = HEADS * NOPE_DIM
    KEYS = G * PAGE
    BLK = 2 * PAGE

    @pl.when(jnp.logical_and(pl.program_id(0) == 0, g == 0))
    def _():
        lhs[0:NK, :] = wkt_ref[...].astype(BF16)

    @pl.when(g == 0)
    def _():
        lhs[NK:NK + HEADS, :] = ql_ref[...]
        m_s[...] = jnp.full_like(m_s, NEG_INF)
        l_s[...] = jnp.zeros_like(l_s)
        acc_s[...] = jnp.zeros_like(acc_s)
        s_s[:, KEYS:] = jnp.full((HEADS, PAGE), NEG_INF, F32)
        lat_s[KEYS:, :] = jnp.zeros((PAGE, KV_LORA), BF16)

    qr = qr_ref[...]

    def scores(lat_b, kpe_t):
        big = _dot_nt(lhs[...], lat_b)
        kn = big[:NK]
        ssq = jnp.sum((kn * kn).reshape(NOPE_DIM, HEADS, kn.shape[1]), axis=0)
        ssq = ssq + jnp.sum(kpe_t * kpe_t, axis=0, keepdims=True)
        raw = big[NK:] + _dot(qr, kpe_t.astype(BF16))
        return raw * lax.rsqrt(ssq * (1.0 / QK_DIM) + NORM_EPS)

    for j in range(0, G, 2):
        lat_b = jnp.concatenate([lat_refs[j][...], lat_refs[j + 1][...]], axis=0).astype(BF16)
        kpe_t = jnp.concatenate([kpe_refs[j][...], kpe_refs[j + 1][...]], axis=1)
        lat_s[j * PAGE:j * PAGE + BLK, :] = lat_b
        s_s[:, j * PAGE:j * PAGE + BLK] = scores(lat_b, kpe_t)

    @pl.when(g == pl.num_programs(1) - 1)
    def _():
        lat_b = nlat_ref[...].astype(BF16)
        col = lax.broadcasted_iota(jnp.int32, (HEADS, PAGE), 1)
        lat_s[KEYS:, :] = lat_b
        s_s[:, KEYS:] = jnp.where(col == 0, scores(lat_b, nkpe_ref[...]), NEG_INF)

    s = s_s[...]
    m = m_s[...]
    m_new = jnp.maximum(m, jnp.max(s, axis=-1, keepdims=True))
    corr = jnp.exp(m - m_new)
    p = jnp.exp(s - m_new)
    l_new = l_s[...] * corr + jnp.sum(p, axis=-1, keepdims=True)
    acc = acc_s[...] * corr + _dot(p.astype(BF16), lat_s[...])
    l_s[...] = l_new
    acc_s[...] = acc
    m_s[...] = m_new

    @pl.when(g == pl.num_programs(1) - 1)
    def _():
        o_ref[...] = acc / l_new


def mla_decode(l, cache_lat, cache_kpe_t, page_table, ql, qr, wk_t, nlat, nkpe_t, G):
    Bd, n_pages = page_table.shape
    assert n_pages % G == 0 and G % 2 == 0
    NK = HEADS * NOPE_DIM
    lat_specs = [pl.BlockSpec((None, None, PAGE, KV_LORA),
                              functools.partial(lambda b, g, pt, j: (l, pt[b, g * G + j], 0, 0), j=j))
                 for j in range(G)]
    kpe_specs = [pl.BlockSpec((None, None, ROPE_DIM, PAGE),
                              functools.partial(lambda b, g, pt, j: (l, pt[b, g * G + j], 0, 0), j=j))
                 for j in range(G)]
    bspec = lambda r, c: pl.BlockSpec((None, r, c), lambda b, g, pt: (b, 0, 0))
    grid_spec = pltpu.PrefetchScalarGridSpec(
        num_scalar_prefetch=1,
        grid=(Bd, n_pages // G),
        in_specs=lat_specs + kpe_specs + [
            bspec(HEADS, KV_LORA), bspec(HEADS, ROPE_DIM),
            pl.BlockSpec((NK, KV_LORA), lambda b, g, pt: (0, 0)),
            bspec(PAGE, KV_LORA), bspec(ROPE_DIM, PAGE)],
        out_specs=bspec(HEADS, KV_LORA),
        scratch_shapes=[pltpu.VMEM((NK + HEADS, KV_LORA), BF16),
                        pltpu.VMEM((HEADS, (G + 1) * PAGE), F32),
                        pltpu.VMEM(((G + 1) * PAGE, KV_LORA), BF16),
                        pltpu.VMEM((HEADS, 1), F32), pltpu.VMEM((HEADS, 1), F32),
                        pltpu.VMEM((HEADS, KV_LORA), F32)],
    )
    return pl.pallas_call(
        functools.partial(_decode_body, G=G),
        grid_spec=grid_spec,
        out_shape=jax.ShapeDtypeStruct((Bd, HEADS, KV_LORA), F32),
        compiler_params=_cparams(("arbitrary", "arbitrary")),
    )(page_table, *([cache_lat] * G), *([cache_kpe_t] * G), ql, qr, wk_t, nlat, nkpe_t)


def _vup_body(a_ref, w_ref, o_ref):
    w = w_ref[...].astype(BF16)
    lane = lax.broadcasted_iota(jnp.int32, o_ref.shape, 1)
    o0 = _dot(a_ref[0].astype(BF16), w[:, :LANES])
    o1 = _dot(a_ref[1].astype(BF16), w[:, LANES:])
    o0 = pltpu.roll(o0, HD, 1)
    o_ref[...] = jnp.where(lane < HD, o0, o1).astype(o_ref.dtype)


def mla_vup(acc_h, wkv2d):
    Bd = acc_h.shape[1]
    return pl.pallas_call(
        _vup_body,
        grid=(HEADS // 2,),
        in_specs=[pl.BlockSpec((2, Bd, KV_LORA), lambda j: (j, 0, 0)),
                  pl.BlockSpec((KV_LORA, 2 * LANES), lambda j: (0, j))],
        out_specs=pl.BlockSpec((Bd, LANES), lambda j: (0, j)),
        out_shape=jax.ShapeDtypeStruct((Bd, HEADS * HD), BF16),
        compiler_params=_cparams(("arbitrary",)),
    )(acc_h, wkv2d)


def _merge_body(ya_ref, yb_ref, yc_ref, w_ref, g0_ref, g1_ref, g2_ref, o_ref, wb):
    @pl.when(pl.program_id(1) == 0)
    def _():
        wb[...] = w_ref[...].astype(BF16)

    acc = g0_ref[...].astype(F32) * _dot(ya_ref[...], wb[0])
    acc = acc + g1_ref[...].astype(F32) * _dot(yb_ref[...], wb[1])
    acc = acc + g2_ref[...].astype(F32) * _dot(yc_ref[...], wb[2])
    o_ref[...] = acc.astype(o_ref.dtype)


def merge_mm(l, ya, yb, yc, wb, gates, tm, tn):
    T, K = ya.shape
    D = wb.shape[-1]
    yspec = pl.BlockSpec((tm, K), lambda j, i: (i, 0))
    return pl.pallas_call(
        _merge_body,
        grid=(D // tn, T // tm),
        in_specs=[yspec, yspec, yspec,
                  pl.BlockSpec((None, N_BRANCH, K, tn), lambda j, i: (l, 0, 0, j))
                  ] + [pl.BlockSpec((tm, tn), functools.partial(lambda j, i, b: (i, b * (D // tn) + j), b=b))
                       for b in range(N_BRANCH)],
        out_specs=pl.BlockSpec((tm, tn), lambda j, i: (i, j)),
        out_shape=jax.ShapeDtypeStruct((T, D), BF16),
        scratch_shapes=[pltpu.VMEM((N_BRANCH, K, tn), BF16)],
        compiler_params=_cparams(("arbitrary", "arbitrary")),
    )(ya, yb, yc, wb, gates, gates, gates)


def _lane_partner(x, sh, lane):
    up = pltpu.roll(x, LANES - sh, 1)
    dn = pltpu.roll(x, sh, 1)
    return jnp.where((lane & sh) == 0, up, dn)


def _seg8(x, op, lane):
    for sh in (1, 2, 4):
        x = op(x, _lane_partner(x, sh, lane))
    return x


def _route_body(lg_ref, bias_ref, gate_o, idx_o):
    lg = lg_ref[...]
    shape = lg.shape
    lane = lax.broadcasted_iota(jnp.int32, shape, 1)
    valid = lane < N_EXPERTS
    s = _sigmoid(lg)
    NINF = -jnp.inf
    sb = jnp.where(valid, s + bias_ref[...], NINF)
    BIG = jnp.int32(1 << 20)
    m1 = _seg8(sb, jnp.maximum, lane)
    i1 = _seg8(jnp.where(sb == m1, lane, BIG), jnp.minimum, lane)
    m2 = _seg8(jnp.where(lane == i1, NINF, sb), jnp.maximum, lane)
    gs = jnp.where(valid, m1 + m2, NINF)
    grp = lane >> 3
    gsel = jnp.zeros(shape, jnp.bool_)
    for _ in range(TOPK_GROUPS):
        mx = jnp.max(gs, axis=-1, keepdims=True)
        first = jnp.min(jnp.where(gs == mx, lane, BIG), axis=-1, keepdims=True)
        hit = grp == (first >> 3)
        gsel = jnp.logical_or(gsel, hit)
        gs = jnp.where(hit, NINF, gs)
    cand = jnp.where(valid, jnp.where(gsel, sb, NEG_INF), NINF)
    sel = jnp.zeros(shape, jnp.bool_)
    idx = jnp.zeros(shape, jnp.int32)
    for it in range(TOP_K):
        mx = jnp.max(cand, axis=-1, keepdims=True)
        first = jnp.min(jnp.where(cand == mx, lane, BIG), axis=-1, keepdims=True)
        hit = lane == first
        sel = jnp.logical_or(sel, hit)
        cand = jnp.where(hit, NINF, cand)
        idx = jnp.where(lane == it, first, idx)
    w = jnp.where(sel, s, 0.0)
    gate_o[...] = w / jnp.sum(w, axis=-1, keepdims=True) * ROUTED_SCALE
    idx_o[...] = idx


def route(logits, bias_pad, tm):
    T = logits.shape[0]
    tile = pl.BlockSpec((tm, LANES), lambda i: (i, 0))
    return pl.pallas_call(
        _route_body,
        grid=(T // tm,),
        in_specs=[tile, pl.BlockSpec((1, LANES), lambda i: (0, 0))],
        out_specs=[tile, tile],
        out_shape=[jax.ShapeDtypeStruct((T, LANES), F32), jax.ShapeDtypeStruct((T, LANES), jnp.int32)],
        compiler_params=_cparams(("arbitrary",)),
    )(logits, bias_pad)


def _expert_body(te_ref, nu_ref, x_ref, wg_ref, wu_ref, wd_ref, rw_ref, o_ref, wg_b, wu_b, wd_b):
    i = pl.program_id(0)
    prev = te_ref[jnp.maximum(i - 1, 0)]
    used = i < nu_ref[0]

    @pl.when(jnp.logical_and(used, jnp.logical_or(i == 0, te_ref[i] != prev)))
    def _():
        wg_b[...] = wg_ref[...].astype(BF16)
        wu_b[...] = wu_ref[...].astype(BF16)
        wd_b[...] = wd_ref[...].astype(BF16)

    @pl.when(used)
    def _():
        x = x_ref[...]
        hid = _silu(_dot(x, wg_b[...])) * _dot(x, wu_b[...])
        y = _dot(hid.astype(BF16), wd_b[...])
        o_ref[...] = (y * rw_ref[...]).astype(o_ref.dtype)

    @pl.when(jnp.logical_not(used))
    def _():
        o_ref[...] = jnp.zeros_like(o_ref)


def expert_ffn(l, tile_expert, n_used, xs, w_gate, w_up, w_down, row_w, tm):
    M, D = xs.shape
    FF = w_gate.shape[-1]
    grid_spec = pltpu.PrefetchScalarGridSpec(
        num_scalar_prefetch=2,
        grid=(M // tm,),
        in_specs=[pl.BlockSpec((tm, D), lambda i, te, nu: (i, 0)),
                  pl.BlockSpec((None, None, D, FF), lambda i, te, nu: (l, te[i], 0, 0)),
                  pl.BlockSpec((None, None, D, FF), lambda i, te, nu: (l, te[i], 0, 0)),
                  pl.BlockSpec((None, None, FF, D), lambda i, te, nu: (l, te[i], 0, 0)),
                  pl.BlockSpec((tm, 1), lambda i, te, nu: (i, 0))],
        out_specs=pl.BlockSpec((tm, D), lambda i, te, nu: (i, 0)),
        scratch_shapes=[pltpu.VMEM((D, FF), BF16), pltpu.VMEM((D, FF), BF16), pltpu.VMEM((FF, D), BF16)],
    )
    return pl.pallas_call(
        _expert_body,
        grid_spec=grid_spec,
        out_shape=jax.ShapeDtypeStruct((M, D), BF16),
        compiler_params=_cparams(("arbitrary",)),
    )(tile_expert, n_used, xs, w_gate, w_up, w_down, row_w)


def _glu_body(x_ref, wg_ref, wu_ref, o_ref, wg_b, wu_b):
    @pl.when(pl.program_id(0) == 0)
    def _():
        wg_b[...] = wg_ref[...].astype(BF16)
        wu_b[...] = wu_ref[...].astype(BF16)

    x = x_ref[...]
    o_ref[...] = (_silu(_dot(x, wg_b[...])) * _dot(x, wu_b[...])).astype(o_ref.dtype)


def glu_mm(l, x, wg, wu, tm):
    T, D = x.shape
    FF = wg.shape[-1]
    wspec = pl.BlockSpec((None, D, FF), lambda i: (l, 0, 0))
    return pl.pallas_call(
        _glu_body,
        grid=(T // tm,),
        in_specs=[pl.BlockSpec((tm, D), lambda i: (i, 0)), wspec, wspec],
        out_specs=pl.BlockSpec((tm, FF), lambda i: (i, 0)),
        out_shape=jax.ShapeDtypeStruct((T, FF), BF16),
        scratch_shapes=[pltpu.VMEM((D, FF), BF16), pltpu.VMEM((D, FF), BF16)],
        compiler_params=_cparams(("arbitrary",)),
    )(x, wg, wu)


def _rope_tables(pos):
    inv = 1.0 / (ROPE_THETA ** (jnp.arange(0, ROPE_DIM, 2, dtype=F32) / ROPE_DIM))
    ang = pos.astype(F32)[:, None] * inv[None, :]
    ang = jnp.concatenate([ang, ang], axis=-1)
    cos, sin = jnp.cos(ang), jnp.sin(ang)
    n = pos.shape[0]
    z64 = jnp.zeros((n, NOPE_DIM), F32)
    z32 = jnp.zeros((n, LANES - QK_DIM), F32)
    return jnp.concatenate([z64, cos, z32, z64, sin, z32], axis=-1)


def _rotate_half_cols(w):
    half = ROPE_DIM // 2
    return jnp.concatenate([-w[..., half:], w[..., :half]], axis=-1)


def _qk_gain_pad(g):
    full = jnp.concatenate([g, g[NOPE_DIM:]], axis=-1)
    return jnp.pad(full, (0, LANES - QK_DIM)).reshape(1, LANES)


def _layer_weights(l, W):
    D = D_MODEL
    o = {}
    w_in = W["w_in"][l]
    c0 = 0
    o["wA"] = w_in[:, c0:c0 + RWKV_IN]; c0 += RWKV_IN
    o["wZ"] = w_in[:, c0:c0 + SSM_DIM]; c0 += SSM_DIM
    o["wX"] = w_in[:, c0:c0 + CONV_DIM]; c0 += CONV_DIM
    w_dt = w_in[:, c0:c0 + HEADS]; c0 += HEADS
    o["wQKV"] = w_in[:, c0:c0 + Q_LORA + KV_LORA]; c0 += Q_LORA + KV_LORA
    w_kpe = w_in[:, c0:c0 + ROPE_DIM]; c0 += ROPE_DIM
    o["wG"] = w_in[:, c0:]
    zpad = lambda n: jnp.zeros((D, n), F32)
    o["wS"] = jnp.concatenate([zpad(NOPE_DIM), w_kpe, zpad(LANES - QK_DIM),
                               zpad(NOPE_DIM), _rotate_half_cols(w_kpe), zpad(LANES - QK_DIM),
                               w_dt, zpad(LANES - HEADS)], axis=1)
    o["mu"] = W["rwkv_mu"][l].reshape(1, RWKV_IN)
    o["vecs8"] = jnp.pad(W["rwkv_vecs"][l], ((0, 1), (0, 0)))
    zl = jnp.zeros((W_LORA, RWKV_DIM), F32)
    o["wup"] = jnp.concatenate([W["rwkv_w_up"][l], zl], axis=0).astype(BF16)
    o["aup"] = jnp.concatenate([zl, W["rwkv_a_up"][l]], axis=0).astype(BF16)
    o["gup"] = W["rwkv_g_up"][l].astype(BF16)
    if l > 0:
        o["v0"] = W["rwkv_v0"][l - 1].reshape(1, RWKV_DIM)
        o["vdn"] = jnp.pad(W["rwkv_v_down"][l - 1], ((0, 0), (0, LANES - V_LORA))).astype(BF16)
        o["vup"] = jnp.pad(W["rwkv_v_up"][l - 1], ((0, LANES - V_LORA), (0, 0))).astype(BF16)
    o["cw8"] = jnp.pad(W["ssm_conv_w"][l], ((0, 8 - CONV_W), (0, 0)))
    o["cb"] = W["ssm_conv_b"][l].reshape(1, CONV_DIM)
    o["dtb"] = jnp.pad(W["ssm_dt_bias"][l], (0, LANES - HEADS)).reshape(1, LANES)
    alog = jnp.pad(W["ssm_a_log"][l], (0, LANES - HEADS))
    o["alog_row"] = alog.reshape(1, LANES)
    o["alog_col"] = alog.reshape(LANES, 1)
    o["dfull"] = jnp.repeat(W["ssm_d"][l], HD).reshape(1, SSM_DIM)
    o["nw"] = W["ssm_norm_w"][l].reshape(1, SSM_DIM)
    wq = W["mla_q_up"][l]
    padh = lambda a: jnp.pad(a, ((0, 0), (0, 0), (0, LANES - a.shape[-1])))
    wq_rot = jnp.concatenate([jnp.zeros_like(wq[..., :NOPE_DIM]), _rotate_half_cols(wq[..., NOPE_DIM:])], axis=-1)
    o["wq2"] = jnp.concatenate([padh(wq).reshape(Q_LORA, -1), padh(wq_rot).reshape(Q_LORA, -1)], axis=1)
    wkv = W["mla_kv_up"][l]
    o["wkv_prep"] = jnp.concatenate([padh(wkv[..., :NOPE_DIM]).reshape(KV_LORA, -1),
                                     wkv[..., NOPE_DIM:].reshape(KV_LORA, -1)], axis=1)
    o["wkv2d"] = wkv.reshape(KV_LORA, HEADS * 2 * HD)
    o["wk_t"] = jnp.transpose(wkv[..., :NOPE_DIM], (2, 1, 0)).reshape(HEADS * NOPE_DIM, KV_LORA)
    o["qn"] = W["mla_q_norm"][l].reshape(1, Q_LORA)
    o["kvn"] = W["mla_kv_norm"][l].reshape(1, KV_LORA)
    o["gq"] = _qk_gain_pad(W["mla_qk_gain_q"][l]) * ATTN_SCALE
    o["gk"] = _qk_gain_pad(W["mla_qk_gain_k"][l])
    o["w_router"] = jnp.pad(W["w_router"][l], ((0, 0), (0, LANES - N_EXPERTS)))
    o["router_bias"] = jnp.pad(W["router_bias"][l], (0, LANES - N_EXPERTS)).reshape(1, LANES)
    return o


def _chain_layout(a, B, L):
    a = a.reshape(B, L, HEADS, HD)
    return jnp.transpose(a, (1, 3, 0, 2)).reshape(L, HD, B * HEADS)


def _mix_block(grp, l, x, mods, st, lw, W, consts, v_first, attend):
    B, L, T, tm = grp.B, grp.L, grp.T, grp.tm
    D = D_MODEL
    sh1, sc1, g1 = mods[0], mods[1], mods[2]
    h = norm_mod(grp, x, W["norm_mix"][l], sc1, sh1)
    zA = mm(h, lw["wA"], tm, RWKV_IN // 2)
    zg = mm(h, lw["wZ"], tm, SSM_DIM)
    xbc = mm(h, lw["wX"], tm, CONV_DIM // 2)
    zq = mm(h, lw["wQKV"], tm, Q_LORA + KV_LORA)
    zsm = mm(h, lw["wS"], tm, 3 * LANES)
    gates = mm(h, lw["wG"], tm, D // 2, epi="sigmoid", out_dtype=BF16)

    shift0, wkv0, conv0, ssm0 = st
    if grp.per_token:
        prev = shift0
        shift_new = zA
    else:
        prev = jnp.zeros((B, 1, RWKV_IN), F32)
        shift_new = zA.reshape(B, L, RWKV_IN)[:, -1]
    l1 = None if l == 0 else (lw["v0"], lw["vdn"], lw["vup"], v_first)
    r, dec, kmod, v, kk, bb, g = rwkv_prep(grp, zA, prev, lw["mu"], lw["vecs8"], lw["wup"], lw["aup"],
                                           lw["gup"], consts["bd64"], l1)
    if l == 0:
        v_first = v
    nchain = B * HEADS
    ns = max(1, LANES // nchain)
    NV = HD // ns
    lay = lambda a: _chain_layout(a, B, L)
    vv = jnp.transpose(v.reshape(B, L, HEADS, ns, NV), (1, 4, 3, 0, 2)).reshape(L, NV, ns * nchain)
    if grp.per_token:
        s0 = jnp.transpose(wkv0.reshape(nchain, HD, HD), (1, 2, 0))
    else:
        s0 = jnp.zeros((NV, HD, ns * nchain), F32)
    TB = 1 if L == 1 else min(L, SCAN_TB)
    y, sT = rwkv_scan(lay(dec), lay(kk), lay(bb), lay(kmod), lay(r), vv, s0, TB)
    y = jnp.transpose(y.reshape(L, NV, ns, B, HEADS), (3, 0, 4, 2, 1)).reshape(T, RWKV_DIM)
    wkv_new = jnp.transpose(sT.reshape(NV, HD, ns, B, HEADS), (3, 4, 2, 0, 1)).reshape(B, HEADS, HD, HD)
    ya = rwkv_post(grp, y, r, kmod, v, g, lw["vecs8"], consts["bd64"])

    xs, bc, dtp = ssd_pre(grp, xbc, conv0, lw["cw8"], lw["cb"], zsm[:, 2 * LANES:], lw["dtb"])
    if grp.per_token:
        conv_new = jnp.stack([conv0[:, 1], conv0[:, 2], xbc], axis=1)
        h0 = jnp.transpose(ssm0.reshape(nchain, HD, SSM_STATE), (1, 2, 0))
        rep = lambda m: jnp.transpose(jnp.repeat(m.reshape(B, SSM_GROUPS, SSM_STATE), HEADS // SSM_GROUPS, axis=1)
                                      .reshape(nchain, SSM_STATE))
        ysd, hn = ssd_step(h0, jnp.transpose(xs.reshape(nchain, HD)), rep(bc[:, :SSM_GROUPS * SSM_STATE]),
                           rep(bc[:, SSM_GROUPS * SSM_STATE:]), dtp[:, :HEADS].reshape(1, nchain),
                           jnp.tile(lw["alog_row"][:, :HEADS], (1, B)))
        ysd = jnp.transpose(ysd).reshape(T, SSM_DIM)
        ssm_new = jnp.transpose(hn, (2, 0, 1)).reshape(B, HEADS, HD, SSM_STATE)
    else:
        conv_new = xbc.reshape(B, L, CONV_DIM)[:, L - (CONV_W - 1):]
        dtpT = jnp.transpose(dtp.reshape(B, L, LANES), (0, 2, 1))
        ysd, sfin = ssd_chunk(B, L, xs, bc, dtp, dtpT, lw["alog_col"], lw["alog_row"], consts["expand"])
        ssm_new = sfin.reshape(B, HEADS, HD, SSM_STATE)
    yb = ssd_post(grp, ysd, xs, zg, lw["dfull"], lw["nw"])

    q, k, vh, lat, kpe = mla_prep(grp, zq, zsm[:, :2 * LANES], consts["cs"], lw["qn"], lw["kvn"],
                                  lw["wq2"], lw["wkv_prep"], lw["gq"], lw["gk"])
    yc = attend(l, q, k, vh, lat, kpe, lw)
    lat_out = lat.reshape(B, L, KV_LORA)
    kpe_out = kpe[:, NOPE_DIM:QK_DIM].reshape(B, L, ROPE_DIM)

    merged = merge_mm(l, ya, yb, yc, W["w_branch"], gates, tm, D // 2)
    x = mm(merged, W["w_out"], tm, D // 2, layer=l, epi="residual",
           extra=(x, g1), extra_specs=(pl.BlockSpec((tm, D // 2), lambda j, i: (i, j)),
                                       grp.vec_spec(D // 2, "ji")))
    states = (lat_out, kpe_out, shift_new, wkv_new, conv_new, ssm_new)
    return x, states, v_first


def _moe(l, h2, lw, W, tm_e):
    T, D = h2.shape
    N = T * TOP_K
    logits = mm(h2, lw["w_router"], TM_MOE, LANES)
    gate, idx = route(logits, lw["router_bias"], TM_MOE)
    eidx = idx[:, :TOP_K]
    wsel = jnp.take_along_axis(gate, eidx, axis=1)
    i32 = jnp.int32
    experts = jnp.arange(N_EXPERTS, dtype=i32)
    flat_e = eidx.reshape(-1)
    order = jnp.argsort(flat_e, stable=True).astype(i32)
    inv = jnp.argsort(order).astype(i32)
    onehot = flat_e[:, None] == experts[None, :]
    counts = jnp.sum(onehot, axis=0, dtype=i32)
    starts = jnp.cumsum(counts) - counts
    pcounts = ((counts + tm_e - 1) // tm_e) * tm_e
    pends = jnp.cumsum(pcounts)
    pstarts = pends - pcounts
    lookup = lambda oh, table: jnp.sum(jnp.where(oh, table[None, :], 0), axis=1, dtype=i32)
    M = -(-N // tm_e) * tm_e + N_EXPERTS * tm_e
    slot = jnp.arange(M, dtype=i32)
    e_slot = jnp.minimum(jnp.sum(pends[None, :] <= slot[:, None], axis=1, dtype=i32), N_EXPERTS - 1)
    oh_slot = e_slot[:, None] == experts[None, :]
    off = slot - lookup(oh_slot, pstarts)
    valid = off < lookup(oh_slot, counts)
    rank = jnp.clip(lookup(oh_slot, starts) + off, 0, N - 1)
    src_flat = jnp.take(order, rank, mode="clip")
    src_tok = jnp.where(valid, src_flat // TOP_K, 0)
    row_w = jnp.where(valid, jnp.take(wsel.reshape(-1), src_flat, mode="clip"), 0.0)
    dest = lookup(onehot, pstarts) + inv - lookup(onehot, starts)
    tile_expert = e_slot[::tm_e]
    n_used = (pends[-1] // tm_e).reshape(1)
    xs = jnp.take(h2, src_tok, axis=0, mode="clip")
    ys = expert_ffn(l, tile_expert, n_used, xs, W["w_exp_gate"], W["w_exp_up"], W["w_exp_down"],
                    row_w.reshape(M, 1), tm_e)
    dest_k = jnp.transpose(dest.reshape(T, TOP_K)).reshape(-1)
    return jnp.take(ys, dest_k, axis=0, mode="clip").reshape(TOP_K, T, D)


def kernel(x_prompt, x_sample, cache_mla_latent, cache_mla_rope, page_table, state_rwkv_shift, state_rwkv_wkv, state_ssm_conv, state_ssm, c_prompt, c_sample, norm_mix, norm_ffn, w_ada, b_ada, w_in, rwkv_mu, rwkv_vecs, rwkv_w_up, rwkv_a_up, rwkv_g_up, rwkv_v0, rwkv_v_down, rwkv_v_up, ssm_conv_w, ssm_conv_b, ssm_dt_bias, ssm_a_log, ssm_d, ssm_norm_w, mla_q_norm, mla_q_up, mla_kv_norm, mla_kv_up, mla_qk_gain_q, mla_qk_gain_k, w_branch, w_out, w_router, router_bias, w_exp_gate, w_exp_up, w_exp_down, w_sh_gate, w_sh_up, w_sh_down):
    W = dict(norm_mix=norm_mix, norm_ffn=norm_ffn, w_ada=w_ada, b_ada=b_ada, w_in=w_in,
             rwkv_mu=rwkv_mu, rwkv_vecs=rwkv_vecs, rwkv_w_up=rwkv_w_up, rwkv_a_up=rwkv_a_up,
             rwkv_g_up=rwkv_g_up, rwkv_v0=rwkv_v0, rwkv_v_down=rwkv_v_down, rwkv_v_up=rwkv_v_up,
             ssm_conv_w=ssm_conv_w, ssm_conv_b=ssm_conv_b, ssm_dt_bias=ssm_dt_bias,
             ssm_a_log=ssm_a_log, ssm_d=ssm_d, ssm_norm_w=ssm_norm_w, mla_q_norm=mla_q_norm,
             mla_q_up=mla_q_up, mla_kv_norm=mla_kv_norm, mla_kv_up=mla_kv_up,
             mla_qk_gain_q=mla_qk_gain_q, mla_qk_gain_k=mla_qk_gain_k, w_branch=w_branch,
             w_out=w_out, w_router=w_router, router_bias=router_bias, w_exp_gate=w_exp_gate,
             w_exp_up=w_exp_up, w_exp_down=w_exp_down, w_sh_gate=w_sh_gate, w_sh_up=w_sh_up,
             w_sh_down=w_sh_down)
    depth = w_in.shape[0]
    D = D_MODEL
    Bp, Lp = x_prompt.shape[0], x_prompt.shape[1]
    Bs, Ls = x_sample.shape[0], x_sample.shape[1]
    assert Ls == 1
    n_pages = page_table.shape[1]
    past_len = n_pages * PAGE
    gp = Group(Bp, Lp, min(TM_PROMPT, Lp))
    gs = Group(Bs, 1, Bs)
    tq = min(TQ_FLASH, Lp)

    ex = np.zeros((LANES, SSM_DIM), np.float32)
    for hh in range(HEADS):
        ex[hh, hh * HD:(hh + 1) * HD] = 1.0
    pm = np.zeros((LANES, ROPE_DIM), np.float32)
    for j in range(ROPE_DIM):
        pm[NOPE_DIM + j, j] = 1.0
    consts_p = dict(bd64=_block_diag01(MXU_DIM, HD), expand=jnp.asarray(ex, BF16),
                    cs=_rope_tables(jnp.arange(Lp, dtype=jnp.int32)))
    consts_s = dict(bd64=consts_p["bd64"], expand=consts_p["expand"],
                    cs=_rope_tables(past_len + jnp.arange(1, dtype=jnp.int32)))
    pm_b = jnp.asarray(pm, BF16)
    G = DECODE_PAGES if n_pages % DECODE_PAGES == 0 else 2
    cache_kpe_t = jnp.swapaxes(cache_mla_rope, 2, 3)

    def attend_prompt(l, q, k, vh, lat, kpe, lw):
        return flash_causal(Bp, Lp, q, k, vh, tq)

    def attend_paged(l, q, k, vh, lat, kpe, lw):
        ql, qr = mla_absorb(q, lw["wkv2d"], lw["gk"], pm_b)
        ql = jnp.transpose(ql, (1, 0, 2))
        qr = jnp.transpose(qr, (1, 0, 2))
        nlat = jnp.pad(lat[:, None, :], ((0, 0), (0, PAGE - 1), (0, 0)))
        nkpe_t = jnp.pad(kpe[:, NOPE_DIM:QK_DIM, None], ((0, 0), (0, 0), (0, PAGE - 1)))
        acc = mla_decode(l, cache_mla_latent, cache_kpe_t, page_table, ql, qr, lw["wk_t"], nlat, nkpe_t, G)
        return mla_vup(jnp.transpose(acc, (1, 0, 2)), lw["wkv2d"])

    xp = x_prompt.reshape(Bp * Lp, D)
    xsm = x_sample.reshape(Bs, D)
    cp8 = jnp.pad(c_prompt, ((0, SUBLANES - Bp % SUBLANES if Bp % SUBLANES else 0), (0, 0)))
    outs_p = [[] for _ in range(6)]
    outs_s = [[] for _ in range(6)]
    vf_p = vf_s = None
    tm_e = TM_EXPERT
    for l in range(depth):
        lw = _layer_weights(l, W)
        bada = b_ada[l].reshape(1, 6 * D)
        bspec = pl.BlockSpec((1, D // 2), lambda j, i: (0, j))
        mod_p = mm(cp8, w_ada, cp8.shape[0], D // 2, layer=l, act_in="silu", epi="bias",
                   extra=(bada,), extra_specs=(bspec,))[:Bp]
        mod_s = mm(c_sample, w_ada, Bs, D // 2, layer=l, act_in="silu", epi="bias",
                   extra=(bada,), extra_specs=(bspec,))
        mods_p = [gp.vec(mod_p[:, i * D:(i + 1) * D]) for i in range(6)]
        mods_s = [gs.vec(mod_s[:, i * D:(i + 1) * D]) for i in range(6)]
        st_s = (state_rwkv_shift[l], state_rwkv_wkv[l], state_ssm_conv[l], state_ssm[l])
        xp, st_p_new, vf_p = _mix_block(gp, l, xp, mods_p, (None, None, None, None), lw, W, consts_p, vf_p,
                                        attend_prompt)
        xsm, st_s_new, vf_s = _mix_block(gs, l, xsm, mods_s, st_s, lw, W, consts_s, vf_s, attend_paged)
        for lst, arr in zip(outs_p, st_p_new):
            lst.append(arr)
        for lst, arr in zip(outs_s, st_s_new):
            lst.append(arr)
        h2p = norm_mod(gp, xp, norm_ffn[l], mods_p[4], mods_p[3])
        h2s = norm_mod(gs, xsm, norm_ffn[l], mods_s[4], mods_s[3])
        h2 = jnp.concatenate([h2p, h2s], axis=0)
        y8 = _moe(l, h2, lw, W, tm_e)
        hid = glu_mm(l, h2, w_sh_gate, w_sh_up, TM_MOE)
        Tp = Bp * Lp
        half = D // 2
        i0s = Tp // gs.tm
        xp = mm(hid, w_sh_down, gp.tm, half, layer=l, rows=Tp, epi="residual_moe",
                extra=(xp, mods_p[5], y8),
                extra_specs=(pl.BlockSpec((gp.tm, half), lambda j, i: (i, j)), gp.vec_spec(half, "ji"),
                             pl.BlockSpec((TOP_K, gp.tm, half), lambda j, i: (0, i, j))))
        xsm = mm(hid, w_sh_down, gs.tm, half, layer=l, rows=Bs, row0=Tp, epi="residual_moe",
                 extra=(xsm, mods_s[5], y8),
                 extra_specs=(pl.BlockSpec((gs.tm, half), lambda j, i: (i, j)), gs.vec_spec(half, "ji"),
                              pl.BlockSpec((TOP_K, gs.tm, half), lambda j, i: (0, i0s + i, j))))
    y_prompt = xp.reshape(Bp, Lp, D)
    y_sample = xsm.reshape(Bs, 1, D)
    sp = [jnp.stack(o) for o in outs_p]
    ss = [jnp.stack(o) for o in outs_s]
    ss[0] = ss[0].reshape(depth, Bs, 1, KV_LORA)
    ss[1] = ss[1].reshape(depth, Bs, 1, ROPE_DIM)
    return (y_prompt, y_sample, sp[0], sp[1], sp[2], sp[3], sp[4], sp[5],
            ss[0], ss[1], ss[2], ss[3], ss[4], ss[5])
```

```python
import functools
import math

import jax
import jax.numpy as jnp
import numpy as np
from jax import lax
from jax.experimental import pallas as pl
from jax.experimental.pallas import tpu as pltpu

F32 = jnp.float32
BF16 = jnp.bfloat16

LANES = 128
SUBLANES = 8
MXU_DIM = 256
VMEM_LIMIT = 56 * 1024 * 1024

D_MODEL = 2048
HEADS = 16
HD = 64
RWKV_DIM = HEADS * HD
W_LORA, A_LORA, V_LORA, G_LORA = 64, 64, 32, 128
RWKV_IN = 3 * RWKV_DIM + W_LORA + A_LORA + G_LORA
RWKV_GN_EPS = 64e-5
SSM_DIM = HEADS * HD
SSM_STATE = 128
SSM_GROUPS = 2
CONV_W = 4
CONV_DIM = SSM_DIM + 2 * SSM_GROUPS * SSM_STATE
SSM_IN = SSM_DIM + CONV_DIM + HEADS
SSD_CHUNK = 128
NOPE_DIM, ROPE_DIM = 64, 32
QK_DIM = NOPE_DIM + ROPE_DIM
Q_LORA, KV_LORA = 512, 256
MLA_IN = Q_LORA + KV_LORA + ROPE_DIM
ROPE_THETA = 10000.0
ATTN_SCALE = QK_DIM ** -0.5
N_BRANCH = 3
N_EXPERTS = 64
TOP_K = 8
N_EXPERT_GROUPS = 8
TOPK_GROUPS = 4
EXPERT_FF = 512
ROUTED_SCALE = 2.5
NORM_EPS = 1e-6
NEG_INF = -1e30
PAGE = 128

TM_PROMPT = 512
TQ_FLASH = 512
FLASH_HEADS = 4
TM_MOE = 512
TM_EXPERT = 512
TM_EXPERT_SMALL = 128
SCAN_TB = 32
DECODE_PAGES = 32


def _cparams(sem):
    return pltpu.CompilerParams(dimension_semantics=sem, vmem_limit_bytes=VMEM_LIMIT)


def _dot(a, b):
    return jnp.dot(a, b, preferred_element_type=F32)


def _dot_nt(a, b):
    return lax.dot_general(a, b, (((1,), (1,)), ((), ())), preferred_element_type=F32)


def _split2(x):
    hi = x.astype(BF16)
    lo = (x - hi.astype(F32)).astype(BF16)
    return hi, lo


def _split3(x):
    hi = x.astype(BF16)
    r = x - hi.astype(F32)
    mid = r.astype(BF16)
    lo = (r - mid.astype(F32)).astype(BF16)
    return hi, mid, lo


def _dot01(x, m01):
    a, b, c = _split3(x)
    return _dot(a, m01) + _dot(b, m01) + _dot(c, m01)


def _segsum(x, bd):
    hi, lo = _split2(x)
    outs = []
    for c in range(x.shape[1] // MXU_DIM):
        sl = slice(c * MXU_DIM, (c + 1) * MXU_DIM)
        outs.append(_dot(hi[:, sl], bd) + _dot(lo[:, sl], bd))
    return outs[0] if len(outs) == 1 else jnp.concatenate(outs, axis=1)


def _softplus(x):
    return jnp.maximum(x, 0.0) + jnp.log1p(jnp.exp(-jnp.abs(x)))


def _sigmoid(x):
    return 1.0 / (1.0 + jnp.exp(-x))


def _silu(x):
    return x * _sigmoid(x)


def _block_diag01(n, seg):
    i = np.arange(n)
    return jnp.asarray((i[:, None] // seg) == (i[None, :] // seg), dtype=BF16)


class Group:
    def __init__(self, batch, seqlen, tm):
        self.B, self.L, self.T, self.tm = batch, seqlen, batch * seqlen, tm
        self.per_token = seqlen == 1
        assert self.T % tm == 0
        assert self.per_token or seqlen % tm == 0
        self.ntiles = self.T // tm

    def vec(self, a):
        return a[None] if self.per_token else a[:, None, :]

    def vec_spec(self, tn, order):
        tm, L = self.tm, self.L
        if self.per_token:
            if order == "ji":
                return pl.BlockSpec((None, tm, tn), lambda j, i: (0, i, j))
            return pl.BlockSpec((None, tm, tn), lambda i: (0, i, 0))
        if order == "ji":
            return pl.BlockSpec((None, 1, tn), lambda j, i: ((i * tm) // L, 0, j))
        return pl.BlockSpec((None, 1, tn), lambda i: ((i * tm) // L, 0, 0))


def _mm_body(*refs, act_in, epi, n_extra):
    a_ref, w_ref = refs[0], refs[1]
    extra = refs[2:2 + n_extra]
    o_ref = refs[2 + n_extra]
    wb_ref = refs[3 + n_extra]

    @pl.when(pl.program_id(1) == 0)
    def _():
        wb_ref[...] = w_ref[...].astype(BF16)

    a = a_ref[...]
    if act_in == "silu":
        a = _silu(a.astype(F32))
    acc = _dot(a.astype(BF16), wb_ref[...])
    if epi == "bias":
        acc = acc + extra[0][...]
    elif epi == "sigmoid":
        acc = _sigmoid(acc)
    elif epi == "residual":
        acc = extra[0][...] + extra[1][...] * acc
    elif epi == "residual_moe":
        routed = extra[2][0].astype(F32)
        for k in range(1, TOP_K):
            routed = routed + extra[2][k].astype(F32)
        acc = extra[0][...] + extra[1][...] * (acc + routed)
    o_ref[...] = acc.astype(o_ref.dtype)


def mm(a, w, tm, tn, *, layer=None, rows=None, row0=0, act_in=None, epi=None, extra=(), extra_specs=(),
       out_dtype=F32):
    K = a.shape[1]
    M = a.shape[0] if rows is None else rows
    N = w.shape[-1]
    assert M % tm == 0 and N % tn == 0 and row0 % tm == 0, (M, tm, N, tn, row0)
    i0 = row0 // tm
    if layer is None:
        w_spec = pl.BlockSpec((K, tn), lambda j, i: (0, j))
    else:
        w_spec = pl.BlockSpec((None, K, tn), lambda j, i: (layer, 0, j))
    body = functools.partial(_mm_body, act_in=act_in, epi=epi, n_extra=len(extra))
    return pl.pallas_call(
        body,
        grid=(N // tn, M // tm),
        in_specs=[pl.BlockSpec((tm, K), lambda j, i: (i0 + i, 0)), w_spec] + list(extra_specs),
        out_specs=pl.BlockSpec((tm, tn), lambda j, i: (i, j)),
        out_shape=jax.ShapeDtypeStruct((M, N), out_dtype),
        scratch_shapes=[pltpu.VMEM((K, tn), BF16)],
        compiler_params=_cparams(("arbitrary", "arbitrary")),
    )(a, w, *extra)


def _norm_mod_body(x_ref, g_ref, sc_ref, sh_ref, *rest, router):
    x = x_ref[...]
    y = x * lax.rsqrt(jnp.mean(x * x, axis=-1, keepdims=True) + NORM_EPS) * g_ref[...]
    h = y * (1.0 + sc_ref[...]) + sh_ref[...]
    if router:
        whi_ref, wlo_ref, o_ref, lg_ref = rest
        hi, lo = _split2(h)
        lg_ref[...] = _dot(hi, whi_ref[...]) + (_dot(hi, wlo_ref[...]) + _dot(lo, whi_ref[...]))
    else:
        o_ref, = rest
    o_ref[...] = h.astype(o_ref.dtype)


def norm_mod(grp, x, g, sc, sh, w_router=None):
    T, D = x.shape
    tm = grp.tm
    router = w_router is not None
    tile = pl.BlockSpec((tm, D), lambda i: (i, 0))
    in_specs = [tile, pl.BlockSpec((1, D), lambda i: (0, 0)), grp.vec_spec(D, "i"), grp.vec_spec(D, "i")]
    args = [x, g.reshape(1, D), sc, sh]
    out_specs, out_shape = tile, jax.ShapeDtypeStruct((T, D), BF16)
    if router:
        w_hi = w_router.astype(BF16)
        w_lo = (w_router - w_hi.astype(F32)).astype(BF16)
        wspec = pl.BlockSpec((D, LANES), lambda i: (0, 0))
        in_specs += [wspec, wspec]
        args += [w_hi, w_lo]
        out_specs = [tile, pl.BlockSpec((tm, LANES), lambda i: (i, 0))]
        out_shape = [out_shape, jax.ShapeDtypeStruct((T, LANES), F32)]
    return pl.pallas_call(
        functools.partial(_norm_mod_body, router=router),
        grid=(T // tm,),
        in_specs=in_specs,
        out_specs=out_specs,
        out_shape=out_shape,
        compiler_params=_cparams(("arbitrary",)),
    )(*args)


def _rwkv_prep_body(*refs, layer1, per_token, tiles_per_seq):
    z_ref, prev_ref, mu_ref, vec_ref, wup_ref, aup_ref, gup_ref, bd_ref = refs[:8]
    rest = refs[8:]
    if layer1:
        v0_ref, vdn_ref, vup_ref, vfirst_ref = rest[:4]
        rest = rest[4:]
    r_o, w_o, k_o, v_o, kk_o, b_o, g_o, carry = rest
    z = z_ref[...]
    tm = z.shape[0]
    if per_token:
        prev = prev_ref[...]
    else:
        first = (pl.program_id(0) % tiles_per_seq) == 0
        prow = jnp.where(first, prev_ref[...], carry[...])
        rid = lax.broadcasted_iota(jnp.int32, z.shape, 0)
        prev = jnp.where(rid == 0, prow, pltpu.roll(z, 1, 0))
        carry[...] = z[tm - 1:tm, :]
    zs = z + mu_ref[...] * (prev - z)
    D = RWKV_DIM
    r, k, v = zs[:, 0:D], zs[:, D:2 * D], zs[:, 2 * D:3 * D]
    lora = zs[:, 3 * D:3 * D + LANES]
    xg = zs[:, 3 * D + LANES:3 * D + 2 * LANES]
    w0, a0 = vec_ref[0:1, :], vec_ref[1:2, :]
    k_k, k_a = vec_ref[2:3, :], vec_ref[3:4, :]
    logw = -_softplus(-(w0 + _dot(jnp.tanh(lora).astype(BF16), wup_ref[...]))) - 0.5
    decay = jnp.exp(-jnp.exp(logw))
    a = _sigmoid(a0 + _dot(lora.astype(BF16), aup_ref[...]))
    g = _dot(_sigmoid(xg).astype(BF16), gup_ref[...])
    if layer1:
        lo = _dot(v.astype(BF16), vdn_ref[...])
        gate_v = _sigmoid(v0_ref[...] + _dot(lo.astype(BF16), vup_ref[...]))
        v = v + (vfirst_ref[...] - v) * gate_v
    kk = k * k_k
    ss = _segsum(kk * kk, bd_ref[...])
    kk = kk / jnp.maximum(jnp.sqrt(ss), 1e-12)
    r_o[...] = r
    w_o[...] = decay
    k_o[...] = k * (1.0 + (a - 1.0) * k_a)
    v_o[...] = v
    kk_o[...] = kk
    b_o[...] = kk * a
    g_o[...] = g


def rwkv_prep(grp, zA, prev, mu, vecs8, wup, aup, gup, bd, layer1_args):
    T = zA.shape[0]
    tm = grp.tm
    D = RWKV_DIM
    layer1 = layer1_args is not None
    full = lambda shp: pl.BlockSpec(shp, lambda i: tuple(0 for _ in shp))
    tile = lambda n: pl.BlockSpec((tm, n), lambda i: (i, 0))
    if grp.per_token:
        prev_spec = tile(RWKV_IN)
        tiles_per_seq = 1
    else:
        tiles_per_seq = grp.L // tm
        prev_spec = pl.BlockSpec((None, 1, RWKV_IN), lambda i: (i // tiles_per_seq, 0, 0))
    in_specs = [tile(RWKV_IN), prev_spec, full((1, RWKV_IN)), full((8, D)),
                full((LANES, D)), full((LANES, D)), full((LANES, D)), full((MXU_DIM, MXU_DIM))]
    args = [zA, prev, mu, vecs8, wup, aup, gup, bd]
    if layer1:
        v0, vdn, vup, vfirst = layer1_args
        in_specs += [full((1, D)), full((D, LANES)), full((LANES, D)), tile(D)]
        args += [v0, vdn, vup, vfirst]
    body = functools.partial(_rwkv_prep_body, layer1=layer1, per_token=grp.per_token,
                             tiles_per_seq=tiles_per_seq)
    return pl.pallas_call(
        body,
        grid=(T // tm,),
        in_specs=in_specs,
        out_specs=[tile(D)] * 7,
        out_shape=[jax.ShapeDtypeStruct((T, D), F32)] * 7,
        scratch_shapes=[pltpu.VMEM((1, RWKV_IN), F32)],
        compiler_params=_cparams(("arbitrary",)),
    )(*args)


def _rwkv_scan_body(w_ref, kk_ref, b_ref, k_ref, r_ref, v_ref, s0_ref, y_ref, sT_ref, S, *dup, TB, NV, ns):
    tb = pl.program_id(1)

    @pl.when(tb == 0)
    def _():
        S[...] = s0_ref[...]

    srcs = (w_ref, kk_ref, b_ref, k_ref, r_ref)
    if ns > 1:
        for src, dst in zip(srcs, dup):
            x = src[...]
            dst[...] = jnp.concatenate([x] * ns, axis=-1)
        srcs = dup

    def step(t, carry):
        w, kk, bb, k, r = (ref[t] for ref in srcs)
        for vi in range(NV):
            s = S[vi]
            sa = jnp.sum(s * kk, axis=0, keepdims=True)
            vv = v_ref[t, pl.ds(vi, 1), :]
            s = s * w - sa * bb + vv * k
            S[vi] = s
            y_ref[t, pl.ds(vi, 1), :] = jnp.sum(s * r, axis=0, keepdims=True)
        return carry

    lax.fori_loop(0, TB, step, 0)

    @pl.when(tb == pl.num_programs(1) - 1)
    def _():
        sT_ref[...] = S[...]


def rwkv_scan(w, kk, b, k, r, v, s0, TB):
    T, NV, LN = v.shape
    ns = HD // NV
    assert T % TB == 0 and LN % LANES == 0 and w.shape[2] * ns == LN
    assert ns == 1 or LN == LANES
    kspec = pl.BlockSpec((TB, HD, LANES // ns), lambda c, t: (t, 0, c))
    vspec = pl.BlockSpec((TB, NV, LANES), lambda c, t: (t, 0, c))
    sspec = pl.BlockSpec((NV, HD, LANES), lambda c, t: (0, 0, c))
    return pl.pallas_call(
        functools.partial(_rwkv_scan_body, TB=TB, NV=NV, ns=ns),
        grid=(LN // LANES, T // TB),
        in_specs=[kspec] * 5 + [vspec, sspec],
        out_specs=[vspec, sspec],
        out_shape=[jax.ShapeDtypeStruct((T, NV, LN), F32), jax.ShapeDtypeStruct((NV, HD, LN), F32)],
        scratch_shapes=[pltpu.VMEM((NV, HD, LANES), F32)]
        + ([pltpu.VMEM((TB, HD, LANES), F32)] * 5 if ns > 1 else []),
        compiler_params=_cparams(("arbitrary", "arbitrary")),
    )(w, kk, b, k, r, v, s0)


def _rwkv_post_body(y_ref, r_ref, k_ref, v_ref, g_ref, vec_ref, bd_ref, o_ref):
    bd = bd_ref[...]
    y = y_ref[...]
    r_k, ln_w, ln_b = vec_ref[4:5, :], vec_ref[5:6, :], vec_ref[6:7, :]
    mean = _segsum(y, bd) * (1.0 / HD)
    d = y - mean
    var = _segsum(d * d, bd) * (1.0 / HD)
    yn = d * lax.rsqrt(var + RWKV_GN_EPS) * ln_w + ln_b
    v = v_ref[...]
    bonus = _segsum(r_ref[...] * k_ref[...] * r_k, bd) * v
    o_ref[...] = ((yn + bonus) * g_ref[...]).astype(o_ref.dtype)


def rwkv_post(grp, y, r, kmod, v, g, vecs8, bd):
    T, D = y.shape
    tm = grp.tm
    tile = pl.BlockSpec((tm, D), lambda i: (i, 0))
    return pl.pallas_call(
        _rwkv_post_body,
        grid=(T // tm,),
        in_specs=[tile] * 5 + [pl.BlockSpec((8, D), lambda i: (0, 0)),
                               pl.BlockSpec((MXU_DIM, MXU_DIM), lambda i: (0, 0))],
        out_specs=tile,
        out_shape=jax.ShapeDtypeStruct((T, D), BF16),
        compiler_params=_cparams(("arbitrary",)),
    )(y, r, kmod, v, g, vecs8, bd)


def _ssd_pre_body(*refs, per_token, tiles_per_seq):
    if per_token:
        x_ref, c0_ref, c1_ref, c2_ref, cw_ref, cb_ref, sm_ref, dtb_ref, xs_o, bc_o, dt_o = refs
        x = x_ref[...]
        taps = [c0_ref[...], c1_ref[...], c2_ref[...], x]
    else:
        x_ref, cw_ref, cb_ref, sm_ref, dtb_ref, xs_o, bc_o, dt_o, carry = refs
        x = x_ref[...]
        tm = x.shape[0]

        @pl.when((pl.program_id(0) % tiles_per_seq) == 0)
        def _():
            carry[...] = jnp.zeros_like(carry)

        ext = jnp.concatenate([carry[...], x], axis=0)
        taps = [pltpu.roll(ext, CONV_W - 1 - i, 0)[SUBLANES:, :] for i in range(CONV_W - 1)] + [x]
        carry[...] = x[tm - SUBLANES:, :]
    conv = cb_ref[...] + taps[0] * cw_ref[0:1, :]
    for i in range(1, CONV_W):
        conv = conv + taps[i] * cw_ref[i:i + 1, :]
    xc = _silu(conv)
    xs_o[...] = xc[:, :SSM_DIM]
    bc_o[...] = xc[:, SSM_DIM:]
    dt_o[...] = _softplus(sm_ref[...] + dtb_ref[...])


def ssd_pre(grp, xbc, conv0, cw8, cb, zsm_dt, dtb):
    T = xbc.shape[0]
    tm = grp.tm
    tile = lambda n: pl.BlockSpec((tm, n), lambda i: (i, 0))
    full = lambda shp: pl.BlockSpec(shp, lambda i: tuple(0 for _ in shp))
    w_specs = [full((8, CONV_DIM)), full((1, CONV_DIM)), tile(LANES), full((1, LANES))]
    if grp.per_token:
        in_specs = [tile(CONV_DIM)] * 4 + w_specs
        args = [xbc, conv0[:, 0], conv0[:, 1], conv0[:, 2], cw8, cb, zsm_dt, dtb]
        scratch = []
        tiles_per_seq = 1
    else:
        in_specs = [tile(CONV_DIM)] + w_specs
        args = [xbc, cw8, cb, zsm_dt, dtb]
        scratch = [pltpu.VMEM((SUBLANES, CONV_DIM), F32)]
        tiles_per_seq = grp.L // tm
    return pl.pallas_call(
        functools.partial(_ssd_pre_body, per_token=grp.per_token, tiles_per_seq=tiles_per_seq),
        grid=(T // tm,),
        in_specs=in_specs,
        out_specs=[tile(SSM_DIM), tile(CONV_DIM - SSM_DIM), tile(LANES)],
        out_shape=[jax.ShapeDtypeStruct((T, SSM_DIM), F32),
                   jax.ShapeDtypeStruct((T, CONV_DIM - SSM_DIM), F32),
                   jax.ShapeDtypeStruct((T, LANES), F32)],
        scratch_shapes=scratch,
        compiler_params=_cparams(("arbitrary",)),
    )(*args)


def _ssd_chunk_body(xs_ref, bc_ref, dtc_ref, dtr_ref, ac_ref, ar_ref, ex_ref, y_ref, sT_ref, St):
    Q = SSD_CHUNK
    c = pl.program_id(1)

    @pl.when(c == 0)
    def _():
        St[...] = jnp.zeros_like(St)

    ri = lax.broadcasted_iota(jnp.int32, (Q, Q), 0)
    ci = lax.broadcasted_iota(jnp.int32, (Q, Q), 1)
    causal = ri >= ci
    tri = jnp.where(causal, 1.0, 0.0).astype(BF16)
    triT = jnp.where(ri <= ci, 1.0, 0.0).astype(BF16)
    dtc = dtc_ref[...]
    a_col = dtc * (-jnp.exp(ac_ref[...]))
    a_row = dtr_ref[...] * (-jnp.exp(ar_ref[...]))
    acs_col = _dot01_lhs(tri, a_col)
    acs_row = _dot01(a_row, triT)
    ex = ex_ref[...]
    dt_full = _dot01(dtc, ex)
    acs_full = _dot01(acs_col, ex)
    xs = xs_ref[...]
    xdt = xs * dt_full
    xdec = xdt * jnp.exp(acs_full[Q - 1:Q, :] - acs_full)
    eacs = jnp.exp(acs_full)
    lane = lax.broadcasted_iota(jnp.int32, (Q, LANES), 1)
    row = lax.broadcasted_iota(jnp.int32, (LANES, LANES), 0)
    xdt_b = xdt.astype(BF16)
    for g in range(SSM_GROUPS):
        bm = bc_ref[:, g * SSM_STATE:(g + 1) * SSM_STATE].astype(BF16)
        cm = bc_ref[:, (SSM_GROUPS + g) * SSM_STATE:(SSM_GROUPS + g + 1) * SSM_STATE].astype(BF16)
        cb = _dot_nt(cm, bm)
        for jp in range(HEADS // SSM_GROUPS // 2):
            j = g * (HEADS // SSM_GROUPS // 2) + jp
            sl = slice(j * LANES, (j + 1) * LANES)
            outs = []
            for hh in range(2):
                h = 2 * j + hh
                diff = acs_col[:, h:h + 1] - acs_row[h:h + 1, :]
                lm = jnp.where(causal, jnp.exp(jnp.minimum(diff, 0.0)), 0.0)
                outs.append(_dot((cb * lm).astype(BF16), xdt_b[:, sl]))
            y_diag = jnp.where(lane < HD, outs[0], outs[1])
            s_old = St[j]
            y_off = _dot_nt(cm, s_old.astype(BF16)) * eacs[:, sl]
            y_ref[:, sl] = y_diag + y_off
            new = _dot(jnp.transpose(xdec[:, sl]).astype(BF16), bm)
            tot0 = jnp.exp(acs_row[2 * j:2 * j + 1, Q - 1:Q])
            tot1 = jnp.exp(acs_row[2 * j + 1:2 * j + 2, Q - 1:Q])
            St[j] = s_old * jnp.where(row < HD, tot0, tot1) + new

    @pl.when(c == pl.num_programs(1) - 1)
    def _():
        sT_ref[...] = St[...]


def _dot01_lhs(m01, x):
    a, b, c = _split3(x)
    return _dot(m01, a) + _dot(m01, b) + _dot(m01, c)


def ssd_chunk(B, L, xs, bc, dtp, dtpT, alog_col, alog_row, expand):
    Q = SSD_CHUNK
    nc = L // Q
    return pl.pallas_call(
        _ssd_chunk_body,
        grid=(B, nc),
        in_specs=[pl.BlockSpec((Q, SSM_DIM), lambda b, c: (b * nc + c, 0)),
                  pl.BlockSpec((Q, CONV_DIM - SSM_DIM), lambda b, c: (b * nc + c, 0)),
                  pl.BlockSpec((Q, LANES), lambda b, c: (b * nc + c, 0)),
                  pl.BlockSpec((None, LANES, Q), lambda b, c: (b, 0, c)),
                  pl.BlockSpec((1, LANES), lambda b, c: (0, 0)),
                  pl.BlockSpec((LANES, 1), lambda b, c: (0, 0)),
                  pl.BlockSpec((LANES, SSM_DIM), lambda b, c: (0, 0))],
        out_specs=[pl.BlockSpec((Q, SSM_DIM), lambda b, c: (b * nc + c, 0)),
                   pl.BlockSpec((None, HEADS // 2, LANES, SSM_STATE), lambda b, c: (b, 0, 0, 0))],
        out_shape=[jax.ShapeDtypeStruct((B * L, SSM_DIM), F32),
                   jax.ShapeDtypeStruct((B, HEADS // 2, LANES, SSM_STATE), F32)],
        scratch_shapes=[pltpu.VMEM((HEADS // 2, LANES, SSM_STATE), F32)],
        compiler_params=_cparams(("arbitrary", "arbitrary")),
    )(xs, bc, dtp, dtpT, alog_row, alog_col, expand)


def _ssd_step_body(h_ref, x_ref, b_ref, c_ref, dt_ref, al_ref, y_ref, hn_ref):
    dt = dt_ref[...]
    dA = jnp.exp(dt * (-jnp.exp(al_ref[...])))
    bm, cm = b_ref[...], c_ref[...]
    for p in range(HD):
        hp = h_ref[p] * dA + (x_ref[p:p + 1, :] * dt) * bm
        hn_ref[p] = hp
        y_ref[p:p + 1, :] = jnp.sum(hp * cm, axis=0, keepdims=True)


def ssd_step(h0, x, bm, cm, dt, alog):
    LN = h0.shape[-1]
    lt = lambda n: pl.BlockSpec((n, LANES), lambda c: (0, c))
    hs = pl.BlockSpec((HD, SSM_STATE, LANES), lambda c: (0, 0, c))
    return pl.pallas_call(
        _ssd_step_body,
        grid=(LN // LANES,),
        in_specs=[hs, lt(HD), lt(SSM_STATE), lt(SSM_STATE), lt(1), lt(1)],
        out_specs=[lt(HD), hs],
        out_shape=[jax.ShapeDtypeStruct((HD, LN), F32), jax.ShapeDtypeStruct((HD, SSM_STATE, LN), F32)],
        compiler_params=_cparams(("arbitrary",)),
    )(h0, x, bm, cm, dt, alog)


def _ssd_post_body(y_ref, xs_ref, zg_ref, d_ref, nw_ref, o_ref):
    y = (y_ref[...] + d_ref[...] * xs_ref[...]) * _silu(zg_ref[...])
    gs = SSM_DIM // SSM_GROUPS
    for g in range(SSM_GROUPS):
        sl = slice(g * gs, (g + 1) * gs)
        yg = y[:, sl]
        n = yg * lax.rsqrt(jnp.mean(yg * yg, axis=-1, keepdims=True) + NORM_EPS)
        o_ref[:, sl] = (n * nw_ref[:, sl]).astype(o_ref.dtype)


def ssd_post(grp, y, xs, zg, dfull, nw):
    T, D = y.shape
    tm = grp.tm
    tile = pl.BlockSpec((tm, D), lambda i: (i, 0))
    row = pl.BlockSpec((1, D), lambda i: (0, 0))
    return pl.pallas_call(
        _ssd_post_body,
        grid=(T // tm,),
        in_specs=[tile, tile, tile, row, row],
        out_specs=tile,
        out_shape=jax.ShapeDtypeStruct((T, D), BF16),
        compiler_params=_cparams(("arbitrary",)),
    )(y, xs, zg, dfull, nw)


def _mla_prep_body(zq_ref, zsm_ref, cs_ref, qn_ref, kvn_ref, wq_ref, wkv_ref, gq_ref, gk_ref,
                   q_o, k_o, v_o, lat_o, kpe_o, wq_b, wkv_b, *, v_transposed):
    @pl.when(pl.program_id(0) == 0)
    def _():
        wq_b[...] = wq_ref[...].astype(BF16)
        wkv_b[...] = wkv_ref[...].astype(BF16)

    HP = HEADS * LANES
    zq = zq_ref[...]
    qd, kvd = zq[:, :Q_LORA], zq[:, Q_LORA:]
    qn = qd * lax.rsqrt(jnp.mean(qd * qd, axis=-1, keepdims=True) + NORM_EPS) * qn_ref[...]
    lat = kvd * lax.rsqrt(jnp.mean(kvd * kvd, axis=-1, keepdims=True) + NORM_EPS) * kvn_ref[...]
    lat_o[...] = lat
    cosv, sinv = cs_ref[:, :LANES], cs_ref[:, LANES:]
    kpe = zsm_ref[:, :LANES] * cosv + zsm_ref[:, LANES:2 * LANES] * sinv
    kpe_o[...] = kpe
    qq = _dot(qn.astype(BF16), wq_b[...])
    kv = _dot(lat.astype(BF16), wkv_b[...])
    if v_transposed:
        for j in range(HEADS // 2):
            v_o[j] = jnp.transpose(kv[:, HP + j * LANES:HP + (j + 1) * LANES]).astype(v_o.dtype)
    else:
        v_o[...] = kv[:, HP:].astype(v_o.dtype)
    gq, gk = gq_ref[...], gk_ref[...]
    nope_one = jnp.where(lax.broadcasted_iota(jnp.int32, (1, LANES), 1) < NOPE_DIM, 1.0, 0.0)
    cq = cosv + nope_one
    for h in range(HEADS):
        sl = slice(h * LANES, (h + 1) * LANES)
        qh = qq[:, sl] * cq + qq[:, HP + h * LANES:HP + (h + 1) * LANES] * sinv
        rq = lax.rsqrt(jnp.sum(qh * qh, axis=-1, keepdims=True) * (1.0 / QK_DIM) + NORM_EPS)
        q_o[:, sl] = (qh * rq * gq).astype(q_o.dtype)
        kh = kv[:, sl] + kpe
        rk = lax.rsqrt(jnp.sum(kh * kh, axis=-1, keepdims=True) * (1.0 / QK_DIM) + NORM_EPS)
        k_o[:, sl] = (kh * rk * gk).astype(k_o.dtype)


def mla_prep(grp, zq, zsm_kpe, cs, qn, kvn, wq2, wkv, gq, gk):
    T = zq.shape[0]
    tm = grp.tm
    HP = HEADS * LANES
    tile = lambda n: pl.BlockSpec((tm, n), lambda i: (i, 0))
    full = lambda shp: pl.BlockSpec(shp, lambda i: tuple(0 for _ in shp))
    if grp.per_token:
        cs_spec = full((1, 2 * LANES))
        v_spec = tile(HEADS * HD)
        v_shape = jax.ShapeDtypeStruct((T, HEADS * HD), BF16)
    else:
        tps = grp.L // tm
        cs_spec = pl.BlockSpec((tm, 2 * LANES), lambda i: (i % tps, 0))
        v_spec = pl.BlockSpec((None, HEADS // 2, None, LANES, tm), lambda i: (i // tps, 0, i % tps, 0, 0))
        v_shape = jax.ShapeDtypeStruct((grp.B, HEADS // 2, tps, LANES, tm), BF16)
    return pl.pallas_call(
        functools.partial(_mla_prep_body, v_transposed=not grp.per_token),
        grid=(T // tm,),
        in_specs=[tile(Q_LORA + KV_LORA), tile(2 * LANES), cs_spec, full((1, Q_LORA)), full((1, KV_LORA)),
                  full((Q_LORA, 2 * HP)), full((KV_LORA, HP + HEADS * HD)), full((1, LANES)), full((1, LANES))],
        out_specs=[tile(HP), tile(HP), v_spec, tile(KV_LORA), tile(LANES)],
        out_shape=[jax.ShapeDtypeStruct((T, HP), BF16), jax.ShapeDtypeStruct((T, HP), BF16),
                   v_shape, jax.ShapeDtypeStruct((T, KV_LORA), F32),
                   jax.ShapeDtypeStruct((T, LANES), F32)],
        scratch_shapes=[pltpu.VMEM((Q_LORA, 2 * HP), BF16), pltpu.VMEM((KV_LORA, HP + HEADS * HD), BF16)],
        compiler_params=_cparams(("arbitrary",)),
    )(zq, zsm_kpe, cs, qn, kvn, wq2, wkv, gq, gk)


def _flash_body(q_ref, k_ref, vt_ref, o_ref, *, tq):
    qi = pl.program_id(2)
    ri = lax.broadcasted_iota(jnp.int32, (tq, tq), 0)
    ci = lax.broadcasted_iota(jnp.int32, (tq, tq), 1)
    NH = FLASH_HEADS
    qs = [q_ref[:, hh * LANES:(hh + 1) * LANES] for hh in range(NH)]

    def block(ki, carry, masked):
        off = pl.multiple_of(ki * tq, tq)
        kb = k_ref[pl.ds(off, tq), :]
        out = []
        for hh in range(NH):
            m, l, acc = carry[hh]
            st = _dot_nt(kb[:, hh * LANES:(hh + 1) * LANES], qs[hh])
            if masked:
                st = jnp.where(ri <= ci, st, NEG_INF)
            m_new = jnp.maximum(m, jnp.max(st, axis=0, keepdims=True))
            corr = jnp.exp(m - m_new)
            pt = jnp.exp(st - m_new)
            l = l * corr + jnp.sum(pt, axis=0, keepdims=True)
            vt = vt_ref[hh // 2, ki, (hh % 2) * HD:(hh % 2 + 1) * HD, :]
            acc = acc * corr + _dot(vt, pt.astype(BF16))
            out.append((m_new, l, acc))
        return tuple(out)

    init = tuple((jnp.full((1, tq), NEG_INF, F32), jnp.zeros((1, tq), F32), jnp.zeros((HD, tq), F32))
                 for _ in range(NH))
    carry = lax.fori_loop(0, qi, lambda ki, c: block(ki, c, False), init)
    carry = block(qi, carry, True)
    ot = jnp.concatenate([c[2] / c[1] for c in carry], axis=0)
    o_ref[...] = jnp.transpose(ot).astype(o_ref.dtype)


def flash_causal(B, L, q, k, vt, tq):
    nq = L // tq
    NH = FLASH_HEADS
    PW = NH * LANES
    assert vt.shape == (B, HEADS // 2, nq, LANES, tq)
    return pl.pallas_call(
        functools.partial(_flash_body, tq=tq),
        grid=(B, HEADS // NH, nq),
        in_specs=[pl.BlockSpec((tq, PW), lambda b, hg, i: (b * nq + i, hg)),
                  pl.BlockSpec((L, PW), lambda b, hg, i: (b, hg)),
                  pl.BlockSpec((None, NH // 2, nq, LANES, tq), lambda b, hg, i: (b, hg, 0, 0, 0))],
        out_specs=pl.BlockSpec((tq, NH * HD), lambda b, hg, i: (b * nq + i, hg)),
        out_shape=jax.ShapeDtypeStruct((B * L, HEADS * HD), BF16),
        compiler_params=_cparams(("arbitrary", "arbitrary", "arbitrary")),
    )(q, k, vt)


def _absorb_body(q_ref, wkv_ref, gk_ref, pm_ref, ql_o, qr_o):
    q = q_ref[...].astype(F32) * gk_ref[...]
    lane = lax.broadcasted_iota(jnp.int32, q.shape, 1)
    qn = jnp.where(lane < NOPE_DIM, q, 0.0).astype(BF16)
    ql_o[...] = _dot_nt(qn, wkv_ref[...].astype(BF16)).astype(ql_o.dtype)
    qr_o[...] = _dot(q.astype(BF16), pm_ref[...]).astype(qr_o.dtype)


def mla_absorb(qpad, wkv2d, gk, pm):
    Bd = qpad.shape[0]
    return pl.pallas_call(
        _absorb_body,
        grid=(HEADS,),
        in_specs=[pl.BlockSpec((Bd, LANES), lambda h: (0, h)),
                  pl.BlockSpec((KV_LORA, LANES), lambda h: (0, h)),
                  pl.BlockSpec((1, LANES), lambda h: (0, 0)),
                  pl.BlockSpec((LANES, ROPE_DIM), lambda h: (0, 0))],
        out_specs=[pl.BlockSpec((None, Bd, KV_LORA), lambda h: (h, 0, 0)),
                   pl.BlockSpec((None, Bd, ROPE_DIM), lambda h: (h, 0, 0))],
        out_shape=[jax.ShapeDtypeStruct((HEADS, Bd, KV_LORA), BF16),
                   jax.ShapeDtypeStruct((HEADS, Bd, ROPE_DIM), BF16)],
        compiler_params=_cparams(("arbitrary",)),
    )(qpad, wkv2d, gk, pm)


def _decode_body(pt_ref, *refs, G):
    lat_refs = refs[:G]
    kpe_refs = refs[G:2 * G]
    ql_ref, qr_ref, wkt_ref, nlat_ref, nkpe_ref, o_ref, lhs, s_s, lat_s, m_s, l_s, acc_s = refs[2 * G:]
    g = pl.program_id(1)
    NK = HEADS * NOPE_DIM
    KEYS = G * PAGE
    BLK = 2 * PAGE

    @pl.when(jnp.logical_and(pl.program_id(0) == 0, g == 0))
    def _():
        lhs[0:NK, :] = wkt_ref[...].astype(BF16)

    @pl.when(g == 0)
    def _():
        lhs[NK:NK + HEADS, :] = ql_ref[...]
        m_s[...] = jnp.full_like(m_s, NEG_INF)
        l_s[...] = jnp.zeros_like(l_s)
        acc_s[...] = jnp.zeros_like(acc_s)
        s_s[:, KEYS:] = jnp.full((HEADS, PAGE), NEG_INF, F32)
        lat_s[KEYS:, :] = jnp.zeros((PAGE, KV_LORA), BF16)

    qr = qr_ref[...]

    def scores(lat_b, kpe_t):
        big = _dot_nt(lhs[...], lat_b)
        kn = big[:NK]
        ssq = jnp.sum((kn * kn).reshape(NOPE_DIM, HEADS, kn.shape[1]), axis=0)
        ssq = ssq + jnp.sum(kpe_t * kpe_t, axis=0, keepdims=True)
        raw = big[NK:] + _dot(qr, kpe_t.astype(BF16))
        return raw * lax.rsqrt(ssq * (1.0 / QK_DIM) + NORM_EPS)

    for j in range(0, G, 2):
        lat_b = jnp.concatenate([lat_refs[j][...], lat_refs[j + 1][...]], axis=0).astype(BF16)
        kpe_t = jnp.concatenate([kpe_refs[j][...], kpe_refs[j + 1][...]], axis=1)
        lat_s[j * PAGE:j * PAGE + BLK, :] = lat_b
        s_s[:, j * PAGE:j * PAGE + BLK] = scores(lat_b, kpe_t)

    @pl.when(g == pl.num_programs(1) - 1)
    def _():
        lat_b = nlat_ref[...].astype(BF16)
        col = lax.broadcasted_iota(jnp.int32, (HEADS, PAGE), 1)
        lat_s[KEYS:, :] = lat_b
        s_s[:, KEYS:] = jnp.where(col == 0, scores(lat_b, nkpe_ref[...]), NEG_INF)

    s = s_s[...]
    m = m_s[...]
    m_new = jnp.maximum(m, jnp.max(s, axis=-1, keepdims=True))
    corr = jnp.exp(m - m_new)
    p = jnp.exp(s - m_new)
    l_new = l_s[...] * corr + jnp.sum(p, axis=-1, keepdims=True)
    acc = acc_s[...] * corr + _dot(p.astype(BF16), lat_s[...])
    l_s[...] = l_new
    acc_s[...] = acc
    m_s[...] = m_new

    @pl.when(g == pl.num_programs(1) - 1)
    def _():
        o_ref[...] = acc / l_new


def mla_decode(l, cache_lat, cache_kpe_t, page_table, ql, qr, wk_t, nlat, nkpe_t, G):
    Bd, n_pages = page_table.shape
    assert n_pages % G == 0 and G % 2 == 0
    NK = HEADS * NOPE_DIM
    lat_specs = [pl.BlockSpec((None, None, PAGE, KV_LORA),
                              functools.partial(lambda b, g, pt, j: (l, pt[b, g * G + j], 0, 0), j=j))
                 for j in range(G)]
    kpe_specs = [pl.BlockSpec((None, None, ROPE_DIM, PAGE),
                              functools.partial(lambda b, g, pt, j: (l, pt[b, g * G + j], 0, 0), j=j))
                 for j in range(G)]
    bspec = lambda r, c: pl.BlockSpec((None, r, c), lambda b, g, pt: (b, 0, 0))
    grid_spec = pltpu.PrefetchScalarGridSpec(
        num_scalar_prefetch=1,
        grid=(Bd, n_pages // G),
        in_specs=lat_specs + kpe_specs + [
            bspec(HEADS, KV_LORA), bspec(HEADS, ROPE_DIM),
            pl.BlockSpec((NK, KV_LORA), lambda b, g, pt: (0, 0)),
            bspec(PAGE, KV_LORA), bspec(ROPE_DIM, PAGE)],
        out_specs=bspec(HEADS, KV_LORA),
        scratch_shapes=[pltpu.VMEM((NK + HEADS, KV_LORA), BF16),
                        pltpu.VMEM((HEADS, (G + 1) * PAGE), F32),
                        pltpu.VMEM(((G + 1) * PAGE, KV_LORA), BF16),
                        pltpu.VMEM((HEADS, 1), F32), pltpu.VMEM((HEADS, 1), F32),
                        pltpu.VMEM((HEADS, KV_LORA), F32)],
    )
    return pl.pallas_call(
        functools.partial(_decode_body, G=G),
        grid_spec=grid_spec,
        out_shape=jax.ShapeDtypeStruct((Bd, HEADS, KV_LORA), F32),
        compiler_params=_cparams(("arbitrary", "arbitrary")),
    )(page_table, *([cache_lat] * G), *([cache_kpe_t] * G), ql, qr, wk_t, nlat, nkpe_t)


def _vup_body(a_ref, w_ref, o_ref):
    w = w_ref[...].astype(BF16)
    lane = lax.broadcasted_iota(jnp.int32, o_ref.shape, 1)
    o0 = _dot(a_ref[0].astype(BF16), w[:, :LANES])
    o1 = _dot(a_ref[1].astype(BF16), w[:, LANES:])
    o0 = pltpu.roll(o0, HD, 1)
    o_ref[...] = jnp.where(lane < HD, o0, o1).astype(o_ref.dtype)


def mla_vup(acc_h, wkv2d):
    Bd = acc_h.shape[1]
    return pl.pallas_call(
        _vup_body,
        grid=(HEADS // 2,),
        in_specs=[pl.BlockSpec((2, Bd, KV_LORA), lambda j: (j, 0, 0)),
                  pl.BlockSpec((KV_LORA, 2 * LANES), lambda j: (0, j))],
        out_specs=pl.BlockSpec((Bd, LANES), lambda j: (0, j)),
        out_shape=jax.ShapeDtypeStruct((Bd, HEADS * HD), BF16),
        compiler_params=_cparams(("arbitrary",)),
    )(acc_h, wkv2d)


def _merge_body(ya_ref, yb_ref, yc_ref, w_ref, g0_ref, g1_ref, g2_ref, o_ref, wb):
    @pl.when(pl.program_id(1) == 0)
    def _():
        wb[...] = w_ref[...].astype(BF16)

    acc = g0_ref[...].astype(F32) * _dot(ya_ref[...], wb[0])
    acc = acc + g1_ref[...].astype(F32) * _dot(yb_ref[...], wb[1])
    acc = acc + g2_ref[...].astype(F32) * _dot(yc_ref[...], wb[2])
    o_ref[...] = acc.astype(o_ref.dtype)


def merge_mm(l, ya, yb, yc, wb, gates, tm, tn):
    T, K = ya.shape
    D = wb.shape[-1]
    yspec = pl.BlockSpec((tm, K), lambda j, i: (i, 0))
    return pl.pallas_call(
        _merge_body,
        grid=(D // tn, T // tm),
        in_specs=[yspec, yspec, yspec,
                  pl.BlockSpec((None, N_BRANCH, K, tn), lambda j, i: (l, 0, 0, j))
                  ] + [pl.BlockSpec((tm, tn), functools.partial(lambda j, i, b: (i, b * (D // tn) + j), b=b))
                       for b in range(N_BRANCH)],
        out_specs=pl.BlockSpec((tm, tn), lambda j, i: (i, j)),
        out_shape=jax.ShapeDtypeStruct((T, D), BF16),
        scratch_shapes=[pltpu.VMEM((N_BRANCH, K, tn), BF16)],
        compiler_params=_cparams(("arbitrary", "arbitrary")),
    )(ya, yb, yc, wb, gates, gates, gates)


def _lane_partner(x, sh, lane):
    up = pltpu.roll(x, LANES - sh, 1)
    dn = pltpu.roll(x, sh, 1)
    return jnp.where((lane & sh) == 0, up, dn)


def _seg8(x, op, lane):
    for sh in (1, 2, 4):
        x = op(x, _lane_partner(x, sh, lane))
    return x


def _route_body(lg_ref, bias_ref, gate_o, idx_o):
    lg = lg_ref[...]
    shape = lg.shape
    lane = lax.broadcasted_iota(jnp.int32, shape, 1)
    valid = lane < N_EXPERTS
    s = _sigmoid(lg)
    NINF = -jnp.inf
    sb = jnp.where(valid, s + bias_ref[...], NINF)
    BIG = jnp.int32(1 << 20)
    m1 = _seg8(sb, jnp.maximum, lane)
    i1 = _seg8(jnp.where(sb == m1, lane, BIG), jnp.minimum, lane)
    m2 = _seg8(jnp.where(lane == i1, NINF, sb), jnp.maximum, lane)
    gs = jnp.where(valid, m1 + m2, NINF)
    grp = lane >> 3
    gsel = jnp.zeros(shape, jnp.bool_)
    for _ in range(TOPK_GROUPS):
        mx = jnp.max(gs, axis=-1, keepdims=True)
        first = jnp.min(jnp.where(gs == mx, lane, BIG), axis=-1, keepdims=True)
        hit = grp == (first >> 3)
        gsel = jnp.logical_or(gsel, hit)
        gs = jnp.where(hit, NINF, gs)
    cand = jnp.where(valid, jnp.where(gsel, sb, NEG_INF), NINF)
    sel = jnp.zeros(shape, jnp.bool_)
    idx = jnp.zeros(shape, jnp.int32)
    for it in range(TOP_K):
        mx = jnp.max(cand, axis=-1, keepdims=True)
        first = jnp.min(jnp.where(cand == mx, lane, BIG), axis=-1, keepdims=True)
        hit = lane == first
        sel = jnp.logical_or(sel, hit)
        cand = jnp.where(hit, NINF, cand)
        idx = jnp.where(lane == it, first, idx)
    w = jnp.where(sel, s, 0.0)
    gate_o[...] = w / jnp.sum(w, axis=-1, keepdims=True) * ROUTED_SCALE
    idx_o[...] = idx


def route(logits, bias_pad, tm):
    T = logits.shape[0]
    tile = pl.BlockSpec((tm, LANES), lambda i: (i, 0))
    return pl.pallas_call(
        _route_body,
        grid=(T // tm,),
        in_specs=[tile, pl.BlockSpec((1, LANES), lambda i: (0, 0))],
        out_specs=[tile, tile],
        out_shape=[jax.ShapeDtypeStruct((T, LANES), F32), jax.ShapeDtypeStruct((T, LANES), jnp.int32)],
        compiler_params=_cparams(("arbitrary",)),
    )(logits, bias_pad)


def _expert_body(te_ref, nu_ref, x_ref, wg_ref, wu_ref, wd_ref, rw_ref, o_ref, wg_b, wu_b, wd_b):
    i = pl.program_id(0)
    prev = te_ref[jnp.maximum(i - 1, 0)]
    used = i < nu_ref[0]

    @pl.when(jnp.logical_and(used, jnp.logical_or(i == 0, te_ref[i] != prev)))
    def _():
        wg_b[...] = wg_ref[...].astype(BF16)
        wu_b[...] = wu_ref[...].astype(BF16)
        wd_b[...] = wd_ref[...].astype(BF16)

    @pl.when(used)
    def _():
        x = x_ref[...]
        hid = _silu(_dot(x, wg_b[...])) * _dot(x, wu_b[...])
        y = _dot(hid.astype(BF16), wd_b[...])
        o_ref[...] = (y * rw_ref[...]).astype(o_ref.dtype)

    @pl.when(jnp.logical_not(used))
    def _():
        o_ref[...] = jnp.zeros_like(o_ref)


def expert_ffn(l, tile_expert, n_used, xs, w_gate, w_up, w_down, row_w, tm):
    M, D = xs.shape
    FF = w_gate.shape[-1]
    grid_spec = pltpu.PrefetchScalarGridSpec(
        num_scalar_prefetch=2,
        grid=(M // tm,),
        in_specs=[pl.BlockSpec((tm, D), lambda i, te, nu: (i, 0)),
                  pl.BlockSpec((None, None, D, FF), lambda i, te, nu: (l, te[i], 0, 0)),
                  pl.BlockSpec((None, None, D, FF), lambda i, te, nu: (l, te[i], 0, 0)),
                  pl.BlockSpec((None, None, FF, D), lambda i, te, nu: (l, te[i], 0, 0)),
                  pl.BlockSpec((tm, 1), lambda i, te, nu: (i, 0))],
        out_specs=pl.BlockSpec((tm, D), lambda i, te, nu: (i, 0)),
        scratch_shapes=[pltpu.VMEM((D, FF), BF16), pltpu.VMEM((D, FF), BF16), pltpu.VMEM((FF, D), BF16)],
    )
    return pl.pallas_call(
        _expert_body,
        grid_spec=grid_spec,
        out_shape=jax.ShapeDtypeStruct((M, D), BF16),
        compiler_params=_cparams(("arbitrary",)),
    )(tile_expert, n_used, xs, w_gate, w_up, w_down, row_w)


def _glu_body(x_ref, wg_ref, wu_ref, o_ref, wg_b, wu_b):
    @pl.when(pl.program_id(0) == 0)
    def _():
        wg_b[...] = wg_ref[...].astype(BF16)
        wu_b[...] = wu_ref[...].astype(BF16)

    x = x_ref[...]
    o_ref[...] = (_silu(_dot(x, wg_b[...])) * _dot(x, wu_b[...])).astype(o_ref.dtype)


def glu_mm(l, x, wg, wu, tm):
    T, D = x.shape
    FF = wg.shape[-1]
    wspec = pl.BlockSpec((None, D, FF), lambda i: (l, 0, 0))
    return pl.pallas_call(
        _glu_body,
        grid=(T // tm,),
        in_specs=[pl.BlockSpec((tm, D), lambda i: (i, 0)), wspec, wspec],
        out_specs=pl.BlockSpec((tm, FF), lambda i: (i, 0)),
        out_shape=jax.ShapeDtypeStruct((T, FF), BF16),
        scratch_shapes=[pltpu.VMEM((D, FF), BF16), pltpu.VMEM((D, FF), BF16)],
        compiler_params=_cparams(("arbitrary",)),
    )(x, wg, wu)


def _rope_tables(pos):
    inv = 1.0 / (ROPE_THETA ** (jnp.arange(0, ROPE_DIM, 2, dtype=F32) / ROPE_DIM))
    ang = pos.astype(F32)[:, None] * inv[None, :]
    ang = jnp.concatenate([ang, ang], axis=-1)
    cos, sin = jnp.cos(ang), jnp.sin(ang)
    n = pos.shape[0]
    z64 = jnp.zeros((n, NOPE_DIM), F32)
    z32 = jnp.zeros((n, LANES - QK_DIM), F32)
    return jnp.concatenate([z64, cos, z32, z64, sin, z32], axis=-1)


def _rotate_half_cols(w):
    half = ROPE_DIM // 2
    return jnp.concatenate([-w[..., half:], w[..., :half]], axis=-1)


def _qk_gain_pad(g):
    full = jnp.concatenate([g, g[NOPE_DIM:]], axis=-1)
    return jnp.pad(full, (0, LANES - QK_DIM)).reshape(1, LANES)


def _layer_weights(l, W):
    D = D_MODEL
    o = {}
    w_in = W["w_in"][l]
    c0 = 0
    o["wA"] = w_in[:, c0:c0 + RWKV_IN]; c0 += RWKV_IN
    o["wZ"] = w_in[:, c0:c0 + SSM_DIM]; c0 += SSM_DIM
    o["wX"] = w_in[:, c0:c0 + CONV_DIM]; c0 += CONV_DIM
    w_dt = w_in[:, c0:c0 + HEADS]; c0 += HEADS
    o["wQKV"] = w_in[:, c0:c0 + Q_LORA + KV_LORA]; c0 += Q_LORA + KV_LORA
    w_kpe = w_in[:, c0:c0 + ROPE_DIM]; c0 += ROPE_DIM
    o["wG"] = w_in[:, c0:]
    zpad = lambda n: jnp.zeros((D, n), F32)
    o["wS"] = jnp.concatenate([zpad(NOPE_DIM), w_kpe, zpad(LANES - QK_DIM),
                               zpad(NOPE_DIM), _rotate_half_cols(w_kpe), zpad(LANES - QK_DIM),
                               w_dt, zpad(LANES - HEADS)], axis=1)
    o["mu"] = W["rwkv_mu"][l].reshape(1, RWKV_IN)
    o["vecs8"] = jnp.pad(W["rwkv_vecs"][l], ((0, 1), (0, 0)))
    zl = jnp.zeros((W_LORA, RWKV_DIM), F32)
    o["wup"] = jnp.concatenate([W["rwkv_w_up"][l], zl], axis=0).astype(BF16)
    o["aup"] = jnp.concatenate([zl, W["rwkv_a_up"][l]], axis=0).astype(BF16)
    o["gup"] = W["rwkv_g_up"][l].astype(BF16)
    if l > 0:
        o["v0"] = W["rwkv_v0"][l - 1].reshape(1, RWKV_DIM)
        o["vdn"] = jnp.pad(W["rwkv_v_down"][l - 1], ((0, 0), (0, LANES - V_LORA))).astype(BF16)
        o["vup"] = jnp.pad(W["rwkv_v_up"][l - 1], ((0, LANES - V_LORA), (0, 0))).astype(BF16)
    o["cw8"] = jnp.pad(W["ssm_conv_w"][l], ((0, 8 - CONV_W), (0, 0)))
    o["cb"] = W["ssm_conv_b"][l].reshape(1, CONV_DIM)
    o["dtb"] = jnp.pad(W["ssm_dt_bias"][l], (0, LANES - HEADS)).reshape(1, LANES)
    alog = jnp.pad(W["ssm_a_log"][l], (0, LANES - HEADS))
    o["alog_row"] = alog.reshape(1, LANES)
    o["alog_col"] = alog.reshape(LANES, 1)
    o["dfull"] = jnp.repeat(W["ssm_d"][l], HD).reshape(1, SSM_DIM)
    o["nw"] = W["ssm_norm_w"][l].reshape(1, SSM_DIM)
    wq = W["mla_q_up"][l]
    padh = lambda a: jnp.pad(a, ((0, 0), (0, 0), (0, LANES - a.shape[-1])))
    wq_rot = jnp.concatenate([jnp.zeros_like(wq[..., :NOPE_DIM]), _rotate_half_cols(wq[..., NOPE_DIM:])], axis=-1)
    o["wq2"] = jnp.concatenate([padh(wq).reshape(Q_LORA, -1), padh(wq_rot).reshape(Q_LORA, -1)], axis=1)
    wkv = W["mla_kv_up"][l]
    o["wkv_prep"] = jnp.concatenate([padh(wkv[..., :NOPE_DIM]).reshape(KV_LORA, -1),
                                     wkv[..., NOPE_DIM:].reshape(KV_LORA, -1)], axis=1)
    o["wkv2d"] = wkv.reshape(KV_LORA, HEADS * 2 * HD)
    o["wk_t"] = jnp.transpose(wkv[..., :NOPE_DIM], (2, 1, 0)).reshape(HEADS * NOPE_DIM, KV_LORA)
    o["qn"] = W["mla_q_norm"][l].reshape(1, Q_LORA)
    o["kvn"] = W["mla_kv_norm"][l].reshape(1, KV_LORA)
    o["gq"] = _qk_gain_pad(W["mla_qk_gain_q"][l]) * ATTN_SCALE
    o["gk"] = _qk_gain_pad(W["mla_qk_gain_k"][l])
    o["w_router"] = jnp.pad(W["w_router"][l], ((0, 0), (0, LANES - N_EXPERTS)))
    o["router_bias"] = jnp.pad(W["router_bias"][l], (0, LANES - N_EXPERTS)).reshape(1, LANES)
    return o


def _chain_layout(a, B, L):
    a = a.reshape(B, L, HEADS, HD)
    return jnp.transpose(a, (1, 3, 0, 2)).reshape(L, HD, B * HEADS)


def _mix_block(grp, l, x, mods, st, lw, W, consts, v_first, attend):
    B, L, T, tm = grp.B, grp.L, grp.T, grp.tm
    D = D_MODEL
    sh1, sc1, g1 = mods[0], mods[1], mods[2]
    h = norm_mod(grp, x, W["norm_mix"][l], sc1, sh1)
    zA = mm(h, lw["wA"], tm, RWKV_IN // 2)
    zg = mm(h, lw["wZ"], tm, SSM_DIM)
    xbc = mm(h, lw["wX"], tm, CONV_DIM // 2)
    zq = mm(h, lw["wQKV"], tm, Q_LORA + KV_LORA)
    zsm = mm(h, lw["wS"], tm, 3 * LANES)
    gates = mm(h, lw["wG"], tm, D // 2, epi="sigmoid")

    shift0, wkv0, conv0, ssm0 = st
    if grp.per_token:
        prev = shift0
        shift_new = zA
    else:
        prev = jnp.zeros((B, 1, RWKV_IN), F32)
        shift_new = zA.reshape(B, L, RWKV_IN)[:, -1]
    l1 = None if l == 0 else (lw["v0"], lw["vdn"], lw["vup"], v_first)
    r, dec, kmod, v, kk, bb, g = rwkv_prep(grp, zA, prev, lw["mu"], lw["vecs8"], lw["wup"], lw["aup"],
                                           lw["gup"], consts["bd64"], l1)
    if l == 0:
        v_first = v
    nchain = B * HEADS
    ns = max(1, LANES // nchain)
    NV = HD // ns
    lay = lambda a: _chain_layout(a, B, L)
    vv = jnp.transpose(v.reshape(B, L, HEADS, ns, NV), (1, 4, 3, 0, 2)).reshape(L, NV, ns * nchain)
    if grp.per_token:
        s0 = jnp.transpose(wkv0.reshape(nchain, HD, HD), (1, 2, 0))
    else:
        s0 = jnp.zeros((NV, HD, ns * nchain), F32)
    TB = 1 if L == 1 else min(L, SCAN_TB)
    y, sT = rwkv_scan(lay(dec), lay(kk), lay(bb), lay(kmod), lay(r), vv, s0, TB)
    y = jnp.transpose(y.reshape(L, NV, ns, B, HEADS), (3, 0, 4, 2, 1)).reshape(T, RWKV_DIM)
    wkv_new = jnp.transpose(sT.reshape(NV, HD, ns, B, HEADS), (3, 4, 2, 0, 1)).reshape(B, HEADS, HD, HD)
    ya = rwkv_post(grp, y, r, kmod, v, g, lw["vecs8"], consts["bd64"])

    xs, bc, dtp = ssd_pre(grp, xbc, conv0, lw["cw8"], lw["cb"], zsm[:, 2 * LANES:], lw["dtb"])
    if grp.per_token:
        conv_new = jnp.stack([conv0[:, 1], conv0[:, 2], xbc], axis=1)
        h0 = jnp.transpose(ssm0.reshape(nchain, HD, SSM_STATE), (1, 2, 0))
        rep = lambda m: jnp.transpose(jnp.repeat(m.reshape(B, SSM_GROUPS, SSM_STATE), HEADS // SSM_GROUPS, axis=1)
                                      .reshape(nchain, SSM_STATE))
        ysd, hn = ssd_step(h0, jnp.transpose(xs.reshape(nchain, HD)), rep(bc[:, :SSM_GROUPS * SSM_STATE]),
                           rep(bc[:, SSM_GROUPS * SSM_STATE:]), dtp[:, :HEADS].reshape(1, nchain),
                           jnp.tile(lw["alog_row"][:, :HEADS], (1, B)))
        ysd = jnp.transpose(ysd).reshape(T, SSM_DIM)
        ssm_new = jnp.transpose(hn, (2, 0, 1)).reshape(B, HEADS, HD, SSM_STATE)
    else:
        conv_new = xbc.reshape(B, L, CONV_DIM)[:, L - (CONV_W - 1):]
        dtpT = jnp.transpose(dtp.reshape(B, L, LANES), (0, 2, 1))
        ysd, sfin = ssd_chunk(B, L, xs, bc, dtp, dtpT, lw["alog_col"], lw["alog_row"], consts["expand"])
        ssm_new = sfin.reshape(B, HEADS, HD, SSM_STATE)
    yb = ssd_post(grp, ysd, xs, zg, lw["dfull"], lw["nw"])

    q, k, vh, lat, kpe = mla_prep(grp, zq, zsm[:, :2 * LANES], consts["cs"], lw["qn"], lw["kvn"],
                                  lw["wq2"], lw["wkv_prep"], lw["gq"], lw["gk"])
    yc = attend(l, q, k, vh, lat, kpe, lw)
    lat_out = lat.reshape(B, L, KV_LORA)
    kpe_out = kpe[:, NOPE_DIM:QK_DIM].reshape(B, L, ROPE_DIM)

    merged = merge_mm(l, ya, yb, yc, W["w_branch"], gates, tm, D // 4)
    x = mm(merged, W["w_out"], tm, D // 2, layer=l, epi="residual",
           extra=(x, g1), extra_specs=(pl.BlockSpec((tm, D // 2), lambda j, i: (i, j)),
                                       grp.vec_spec(D // 2, "ji")))
    states = (lat_out, kpe_out, shift_new, wkv_new, conv_new, ssm_new)
    return x, states, v_first


def _moe(l, h2, logits, lw, W, tm_r, tm_e):
    T, D = h2.shape
    N = T * TOP_K
    gate, idx = route(logits, lw["router_bias"], tm_r)
    eidx = idx[:, :TOP_K]
    wsel = jnp.take_along_axis(gate, eidx, axis=1)
    i32 = jnp.int32
    experts = jnp.arange(N_EXPERTS, dtype=i32)
    flat_e = eidx.reshape(-1)
    order = jnp.argsort(flat_e, stable=True).astype(i32)
    inv = jnp.argsort(order).astype(i32)
    onehot = flat_e[:, None] == experts[None, :]
    counts = jnp.sum(onehot, axis=0, dtype=i32)
    starts = jnp.cumsum(counts) - counts
    pcounts = ((counts + tm_e - 1) // tm_e) * tm_e
    pends = jnp.cumsum(pcounts)
    pstarts = pends - pcounts
    lookup = lambda oh, table: jnp.sum(jnp.where(oh, table[None, :], 0), axis=1, dtype=i32)
    M = -(-N // tm_e) * tm_e + N_EXPERTS * tm_e
    slot = jnp.arange(M, dtype=i32)
    e_slot = jnp.minimum(jnp.sum(pends[None, :] <= slot[:, None], axis=1, dtype=i32), N_EXPERTS - 1)
    oh_slot = e_slot[:, None] == experts[None, :]
    off = slot - lookup(oh_slot, pstarts)
    valid = off < lookup(oh_slot, counts)
    rank = jnp.clip(lookup(oh_slot, starts) + off, 0, N - 1)
    src_flat = jnp.take(order, rank, mode="clip")
    src_tok = jnp.where(valid, src_flat // TOP_K, 0)
    row_w = jnp.where(valid, jnp.take(wsel.reshape(-1), src_flat, mode="clip"), 0.0)
    dest = lookup(onehot, pstarts) + inv - lookup(onehot, starts)
    tile_expert = e_slot[::tm_e]
    n_used = (pends[-1] // tm_e).reshape(1)
    xs = jnp.take(h2, src_tok, axis=0, mode="clip")
    ys = expert_ffn(l, tile_expert, n_used, xs, W["w_exp_gate"], W["w_exp_up"], W["w_exp_down"],
                    row_w.reshape(M, 1), tm_e)
    dest_k = jnp.transpose(dest.reshape(T, TOP_K)).reshape(-1)
    return jnp.take(ys, dest_k, axis=0, mode="clip").reshape(TOP_K, T, D)


def kernel(x_prompt, x_sample, cache_mla_latent, cache_mla_rope, page_table, state_rwkv_shift, state_rwkv_wkv, state_ssm_conv, state_ssm, c_prompt, c_sample, norm_mix, norm_ffn, w_ada, b_ada, w_in, rwkv_mu, rwkv_vecs, rwkv_w_up, rwkv_a_up, rwkv_g_up, rwkv_v0, rwkv_v_down, rwkv_v_up, ssm_conv_w, ssm_conv_b, ssm_dt_bias, ssm_a_log, ssm_d, ssm_norm_w, mla_q_norm, mla_q_up, mla_kv_norm, mla_kv_up, mla_qk_gain_q, mla_qk_gain_k, w_branch, w_out, w_router, router_bias, w_exp_gate, w_exp_up, w_exp_down, w_sh_gate, w_sh_up, w_sh_down):
    W = dict(norm_mix=norm_mix, norm_ffn=norm_ffn, w_ada=w_ada, b_ada=b_ada, w_in=w_in,
             rwkv_mu=rwkv_mu, rwkv_vecs=rwkv_vecs, rwkv_w_up=rwkv_w_up, rwkv_a_up=rwkv_a_up,
             rwkv_g_up=rwkv_g_up, rwkv_v0=rwkv_v0, rwkv_v_down=rwkv_v_down, rwkv_v_up=rwkv_v_up,
             ssm_conv_w=ssm_conv_w, ssm_conv_b=ssm_conv_b, ssm_dt_bias=ssm_dt_bias,
             ssm_a_log=ssm_a_log, ssm_d=ssm_d, ssm_norm_w=ssm_norm_w, mla_q_norm=mla_q_norm,
             mla_q_up=mla_q_up, mla_kv_norm=mla_kv_norm, mla_kv_up=mla_kv_up,
             mla_qk_gain_q=mla_qk_gain_q, mla_qk_gain_k=mla_qk_gain_k, w_branch=w_branch,
             w_out=w_out, w_router=w_router, router_bias=router_bias, w_exp_gate=w_exp_gate,
             w_exp_up=w_exp_up, w_exp_down=w_exp_down, w_sh_gate=w_sh_gate, w_sh_up=w_sh_up,
             w_sh_down=w_sh_down)
    depth = w_in.shape[0]
    D = D_MODEL
    Bp, Lp = x_prompt.shape[0], x_prompt.shape[1]
    Bs, Ls = x_sample.shape[0], x_sample.shape[1]
    assert Ls == 1
    n_pages = page_table.shape[1]
    past_len = n_pages * PAGE
    gp = Group(Bp, Lp, min(TM_PROMPT, Lp))
    gs = Group(Bs, 1, Bs)
    tq = min(TQ_FLASH, Lp)

    ex = np.zeros((LANES, SSM_DIM), np.float32)
    for hh in range(HEADS):
        ex[hh, hh * HD:(hh + 1) * HD] = 1.0
    pm = np.zeros((LANES, ROPE_DIM), np.float32)
    for j in range(ROPE_DIM):
        pm[NOPE_DIM + j, j] = 1.0
    consts_p = dict(bd64=_block_diag01(MXU_DIM, HD), expand=jnp.asarray(ex, BF16),
                    cs=_rope_tables(jnp.arange(Lp, dtype=jnp.int32)))
    consts_s = dict(bd64=consts_p["bd64"], expand=consts_p["expand"],
                    cs=_rope_tables(past_len + jnp.arange(1, dtype=jnp.int32)))
    pm_b = jnp.asarray(pm, BF16)
    G = DECODE_PAGES if n_pages % DECODE_PAGES == 0 else 2
    cache_kpe_t = jnp.swapaxes(cache_mla_rope, 2, 3)

    def attend_prompt(l, q, k, vh, lat, kpe, lw):
        return flash_causal(Bp, Lp, q, k, vh, tq)

    def attend_paged(l, q, k, vh, lat, kpe, lw):
        ql, qr = mla_absorb(q, lw["wkv2d"], lw["gk"], pm_b)
        ql = jnp.transpose(ql, (1, 0, 2))
        qr = jnp.transpose(qr, (1, 0, 2))
        nlat = jnp.pad(lat[:, None, :], ((0, 0), (0, PAGE - 1), (0, 0)))
        nkpe_t = jnp.pad(kpe[:, NOPE_DIM:QK_DIM, None], ((0, 0), (0, 0), (0, PAGE - 1)))
        acc = mla_decode(l, cache_mla_latent, cache_kpe_t, page_table, ql, qr, lw["wk_t"], nlat, nkpe_t, G)
        return mla_vup(jnp.transpose(acc, (1, 0, 2)), lw["wkv2d"])

    xp = x_prompt.reshape(Bp * Lp, D)
    xsm = x_sample.reshape(Bs, D)
    cp8 = jnp.pad(c_prompt, ((0, SUBLANES - Bp % SUBLANES if Bp % SUBLANES else 0), (0, 0)))
    outs_p = [[] for _ in range(6)]
    outs_s = [[] for _ in range(6)]
    vf_p = vf_s = None
    assert tq == gp.tm
    for l in range(depth):
        lw = _layer_weights(l, W)
        bada = b_ada[l].reshape(1, 6 * D)
        bspec = pl.BlockSpec((1, D // 2), lambda j, i: (0, j))
        mod_p = mm(cp8, w_ada, cp8.shape[0], D // 2, layer=l, act_in="silu", epi="bias",
                   extra=(bada,), extra_specs=(bspec,))[:Bp]
        mod_s = mm(c_sample, w_ada, Bs, D // 2, layer=l, act_in="silu", epi="bias",
                   extra=(bada,), extra_specs=(bspec,))
        mods_p = [gp.vec(mod_p[:, i * D:(i + 1) * D]) for i in range(6)]
        mods_s = [gs.vec(mod_s[:, i * D:(i + 1) * D]) for i in range(6)]
        st_s = (state_rwkv_shift[l], state_rwkv_wkv[l], state_ssm_conv[l], state_ssm[l])
        xp, st_p_new, vf_p = _mix_block(gp, l, xp, mods_p, (None, None, None, None), lw, W, consts_p, vf_p,
                                        attend_prompt)
        xsm, st_s_new, vf_s = _mix_block(gs, l, xsm, mods_s, st_s, lw, W, consts_s, vf_s, attend_paged)
        for lst, arr in zip(outs_p, st_p_new):
            lst.append(arr)
        for lst, arr in zip(outs_s, st_s_new):
            lst.append(arr)
        half = D // 2

        def ffn(grp, x, mods, tm_r, tm_e):
            h2, logits = norm_mod(grp, x, norm_ffn[l], mods[4], mods[3], lw["w_router"])
            y8 = _moe(l, h2, logits, lw, W, tm_r, tm_e)
            hid = glu_mm(l, h2, w_sh_gate, w_sh_up, tm_r)
            return mm(hid, w_sh_down, grp.tm, half, layer=l, epi="residual_moe",
                      extra=(x, mods[5], y8),
                      extra_specs=(pl.BlockSpec((grp.tm, half), lambda j, i: (i, j)), grp.vec_spec(half, "ji"),
                                   pl.BlockSpec((TOP_K, grp.tm, half), lambda j, i: (0, i, j))))

        xp = ffn(gp, xp, mods_p, min(TM_MOE, gp.T), TM_EXPERT)
        xsm = ffn(gs, xsm, mods_s, min(TM_MOE, gs.T), min(TM_EXPERT_SMALL, TM_EXPERT))
        xp, xsm = lax.optimization_barrier((xp, xsm))
    y_prompt = xp.reshape(Bp, Lp, D)
    y_sample = xsm.reshape(Bs, 1, D)
    sp = [jnp.stack(o) for o in outs_p]
    ss = [jnp.stack(o) for o in outs_s]
    ss[0] = ss[0].reshape(depth, Bs, 1, KV_LORA)
    ss[1] = ss[1].reshape(depth, Bs, 1, ROPE_DIM)
    return (y_prompt, y_sample, sp[0], sp[1], sp[2], sp[3], sp[4], sp[5],
            ss[0], ss[1], ss[2], ss[3], ss[4], ss[5])
```

```python
import functools
import math

import jax
import jax.numpy as jnp
import numpy as np
from jax import lax
from jax.experimental import pallas as pl
from jax.experimental.pallas import tpu as pltpu

F32 = jnp.float32
BF16 = jnp.bfloat16

LANES = 128
SUBLANES = 8
MXU_DIM = 256
VMEM_LIMIT = 56 * 1024 * 1024

D_MODEL = 2048
HEADS = 16
HD = 64
RWKV_DIM = HEADS * HD
W_LORA, A_LORA, V_LORA, G_LORA = 64, 64, 32, 128
RWKV_IN = 3 * RWKV_DIM + W_LORA + A_LORA + G_LORA
RWKV_GN_EPS = 64e-5
SSM_DIM = HEADS * HD
SSM_STATE = 128
SSM_GROUPS = 2
CONV_W = 4
CONV_DIM = SSM_DIM + 2 * SSM_GROUPS * SSM_STATE
SSM_IN = SSM_DIM + CONV_DIM + HEADS
SSD_CHUNK = 128
NOPE_DIM, ROPE_DIM = 64, 32
QK_DIM = NOPE_DIM + ROPE_DIM
Q_LORA, KV_LORA = 512, 256
MLA_IN = Q_LORA + KV_LORA + ROPE_DIM
ROPE_THETA = 10000.0
ATTN_SCALE = QK_DIM ** -0.5
N_BRANCH = 3
N_EXPERTS = 64
TOP_K = 8
N_EXPERT_GROUPS = 8
TOPK_GROUPS = 4
EXPERT_FF = 512
ROUTED_SCALE = 2.5
NORM_EPS = 1e-6
NEG_INF = -1e30
PAGE = 128

TM_PROMPT = 512
TQ_FLASH = 512
FLASH_HEADS = 4
TM_MOE = 512
TM_EXPERT = 512
TM_EXPERT_SMALL = 128
SCAN_TB = 32
DECODE_PAGES = 32


def _cparams(sem):
    return pltpu.CompilerParams(dimension_semantics=sem, vmem_limit_bytes=VMEM_LIMIT)


def _dot(a, b):
    return jnp.dot(a, b, preferred_element_type=F32)


def _dot_nt(a, b):
    return lax.dot_general(a, b, (((1,), (1,)), ((), ())), preferred_element_type=F32)


def _split2(x):
    hi = x.astype(BF16)
    lo = (x - hi.astype(F32)).astype(BF16)
    return hi, lo


def _split3(x):
    hi = x.astype(BF16)
    r = x - hi.astype(F32)
    mid = r.astype(BF16)
    lo = (r - mid.astype(F32)).astype(BF16)
    return hi, mid, lo


def _dot01(x, m01):
    a, b, c = _split3(x)
    return _dot(a, m01) + _dot(b, m01) + _dot(c, m01)


def _segsum(x, bd):
    hi, lo = _split2(x)
    outs = []
    for c in range(x.shape[1] // MXU_DIM):
        sl = slice(c * MXU_DIM, (c + 1) * MXU_DIM)
        outs.append(_dot(hi[:, sl], bd) + _dot(lo[:, sl], bd))
    return outs[0] if len(outs) == 1 else jnp.concatenate(outs, axis=1)


def _softplus(x):
    return jnp.maximum(x, 0.0) + jnp.log1p(jnp.exp(-jnp.abs(x)))


def _sigmoid(x):
    return 1.0 / (1.0 + jnp.exp(-x))


def _silu(x):
    return x * _sigmoid(x)


def _block_diag01(n, seg):
    i = np.arange(n)
    return jnp.asarray((i[:, None] // seg) == (i[None, :] // seg), dtype=BF16)


class Group:
    def __init__(self, batch, seqlen, tm):
        self.B, self.L, self.T, self.tm = batch, seqlen, batch * seqlen, tm
        self.per_token = seqlen == 1
        assert self.T % tm == 0
        assert self.per_token or seqlen % tm == 0
        self.ntiles = self.T // tm

    def vec(self, a):
        return a[None] if self.per_token else a[:, None, :]

    def vec_spec(self, tn, order):
        tm, L = self.tm, self.L
        if self.per_token:
            if order == "ji":
                return pl.BlockSpec((None, tm, tn), lambda j, i: (0, i, j))
            return pl.BlockSpec((None, tm, tn), lambda i: (0, i, 0))
        if order == "ji":
            return pl.BlockSpec((None, 1, tn), lambda j, i: ((i * tm) // L, 0, j))
        return pl.BlockSpec((None, 1, tn), lambda i: ((i * tm) // L, 0, 0))


def _mm_body(*refs, act_in, epi, n_extra):
    a_ref, w_ref = refs[0], refs[1]
    extra = refs[2:2 + n_extra]
    o_ref = refs[2 + n_extra]
    wb_ref = refs[3 + n_extra]

    @pl.when(pl.program_id(1) == 0)
    def _():
        wb_ref[...] = w_ref[...].astype(BF16)

    a = a_ref[...]
    if act_in == "silu":
        a = _silu(a.astype(F32))
    acc = _dot(a.astype(BF16), wb_ref[...])
    if epi == "bias":
        acc = acc + extra[0][...]
    elif epi == "sigmoid":
        acc = _sigmoid(acc)
    elif epi == "residual":
        acc = extra[0][...] + extra[1][...] * acc
    elif epi == "residual_moe":
        routed = extra[2][0].astype(F32)
        for k in range(1, TOP_K):
            routed = routed + extra[2][k].astype(F32)
        acc = extra[0][...] + extra[1][...] * (acc + routed)
    o_ref[...] = acc.astype(o_ref.dtype)


def mm(a, w, tm, tn, *, layer=None, rows=None, row0=0, act_in=None, epi=None, extra=(), extra_specs=(),
       out_dtype=F32):
    K = a.shape[1]
    M = a.shape[0] if rows is None else rows
    N = w.shape[-1]
    assert M % tm == 0 and N % tn == 0 and row0 % tm == 0, (M, tm, N, tn, row0)
    i0 = row0 // tm
    if layer is None:
        w_spec = pl.BlockSpec((K, tn), lambda j, i: (0, j))
    else:
        w_spec = pl.BlockSpec((None, K, tn), lambda j, i: (layer, 0, j))
    body = functools.partial(_mm_body, act_in=act_in, epi=epi, n_extra=len(extra))
    return pl.pallas_call(
        body,
        grid=(N // tn, M // tm),
        in_specs=[pl.BlockSpec((tm, K), lambda j, i: (i0 + i, 0)), w_spec] + list(extra_specs),
        out_specs=pl.BlockSpec((tm, tn), lambda j, i: (i, j)),
        out_shape=jax.ShapeDtypeStruct((M, N), out_dtype),
        scratch_shapes=[pltpu.VMEM((K, tn), BF16)],
        compiler_params=_cparams(("arbitrary", "arbitrary")),
    )(a, w, *extra)


def _norm_mod_body(x_ref, g_ref, sc_ref, sh_ref, *rest, router):
    x = x_ref[...]
    y = x * lax.rsqrt(jnp.mean(x * x, axis=-1, keepdims=True) + NORM_EPS) * g_ref[...]
    h = y * (1.0 + sc_ref[...]) + sh_ref[...]
    if router:
        whi_ref, wlo_ref, o_ref, lg_ref = rest
        hi, lo = _split2(h)
        lg_ref[...] = _dot(hi, whi_ref[...]) + (_dot(hi, wlo_ref[...]) + _dot(lo, whi_ref[...]))
    else:
        o_ref, = rest
    o_ref[...] = h.astype(o_ref.dtype)


def norm_mod(grp, x, g, sc, sh, w_router=None):
    T, D = x.shape
    tm = grp.tm
    router = w_router is not None
    tile = pl.BlockSpec((tm, D), lambda i: (i, 0))
    in_specs = [tile, pl.BlockSpec((1, D), lambda i: (0, 0)), grp.vec_spec(D, "i"), grp.vec_spec(D, "i")]
    args = [x, g.reshape(1, D), sc, sh]
    out_specs, out_shape = tile, jax.ShapeDtypeStruct((T, D), BF16)
    if router:
        w_hi = w_router.astype(BF16)
        w_lo = (w_router - w_hi.astype(F32)).astype(BF16)
        wspec = pl.BlockSpec((D, LANES), lambda i: (0, 0))
        in_specs += [wspec, wspec]
        args += [w_hi, w_lo]
        out_specs = [tile, pl.BlockSpec((tm, LANES), lambda i: (i, 0))]
        out_shape = [out_shape, jax.ShapeDtypeStruct((T, LANES), F32)]
    return pl.pallas_call(
        functools.partial(_norm_mod_body, router=router),
        grid=(T // tm,),
        in_specs=in_specs,
        out_specs=out_specs,
        out_shape=out_shape,
        compiler_params=_cparams(("arbitrary",)),
    )(*args)


def _rwkv_prep_body(*refs, layer1, per_token, tiles_per_seq):
    z_ref, prev_ref, mu_ref, vec_ref, wup_ref, aup_ref, gup_ref, bd_ref = refs[:8]
    rest = refs[8:]
    if layer1:
        v0_ref, vdn_ref, vup_ref, vfirst_ref = rest[:4]
        rest = rest[4:]
    r_o, w_o, k_o, v_o, kk_o, b_o, g_o, carry = rest
    z = z_ref[...]
    tm = z.shape[0]
    if per_token:
        prev = prev_ref[...]
    else:
        first = (pl.program_id(0) % tiles_per_seq) == 0
        prow = jnp.where(first, prev_ref[...], carry[...])
        rid = lax.broadcasted_iota(jnp.int32, z.shape, 0)
        prev = jnp.where(rid == 0, prow, pltpu.roll(z, 1, 0))
        carry[...] = z[tm - 1:tm, :]
    zs = z + mu_ref[...] * (prev - z)
    D = RWKV_DIM
    r, k, v = zs[:, 0:D], zs[:, D:2 * D], zs[:, 2 * D:3 * D]
    lora = zs[:, 3 * D:3 * D + LANES]
    xg = zs[:, 3 * D + LANES:3 * D + 2 * LANES]
    w0, a0 = vec_ref[0:1, :], vec_ref[1:2, :]
    k_k, k_a = vec_ref[2:3, :], vec_ref[3:4, :]
    logw = -_softplus(-(w0 + _dot(jnp.tanh(lora).astype(BF16), wup_ref[...]))) - 0.5
    decay = jnp.exp(-jnp.exp(logw))
    a = _sigmoid(a0 + _dot(lora.astype(BF16), aup_ref[...]))
    g = _dot(_sigmoid(xg).astype(BF16), gup_ref[...])
    if layer1:
        lo = _dot(v.astype(BF16), vdn_ref[...])
        gate_v = _sigmoid(v0_ref[...] + _dot(lo.astype(BF16), vup_ref[...]))
        v = v + (vfirst_ref[...] - v) * gate_v
    kk = k * k_k
    ss = _segsum(kk * kk, bd_ref[...])
    kk = kk / jnp.maximum(jnp.sqrt(ss), 1e-12)
    r_o[...] = r
    w_o[...] = decay
    k_o[...] = k * (1.0 + (a - 1.0) * k_a)
    v_o[...] = v
    kk_o[...] = kk
    b_o[...] = kk * a
    g_o[...] = g


def rwkv_prep(grp, zA, prev, mu, vecs8, wup, aup, gup, bd, layer1_args):
    T = zA.shape[0]
    tm = grp.tm
    D = RWKV_DIM
    layer1 = layer1_args is not None
    full = lambda shp: pl.BlockSpec(shp, lambda i: tuple(0 for _ in shp))
    tile = lambda n: pl.BlockSpec((tm, n), lambda i: (i, 0))
    if grp.per_token:
        prev_spec = tile(RWKV_IN)
        tiles_per_seq = 1
    else:
        tiles_per_seq = grp.L // tm
        prev_spec = pl.BlockSpec((None, 1, RWKV_IN), lambda i: (i // tiles_per_seq, 0, 0))
    in_specs = [tile(RWKV_IN), prev_spec, full((1, RWKV_IN)), full((8, D)),
                full((LANES, D)), full((LANES, D)), full((LANES, D)), full((MXU_DIM, MXU_DIM))]
    args = [zA, prev, mu, vecs8, wup, aup, gup, bd]
    if layer1:
        v0, vdn, vup, vfirst = layer1_args
        in_specs += [full((1, D)), full((D, LANES)), full((LANES, D)), tile(D)]
        args += [v0, vdn, vup, vfirst]
    body = functools.partial(_rwkv_prep_body, layer1=layer1, per_token=grp.per_token,
                             tiles_per_seq=tiles_per_seq)
    return pl.pallas_call(
        body,
        grid=(T // tm,),
        in_specs=in_specs,
        out_specs=[tile(D)] * 7,
        out_shape=[jax.ShapeDtypeStruct((T, D), F32)] * 7,
        scratch_shapes=[pltpu.VMEM((1, RWKV_IN), F32)],
        compiler_params=_cparams(("arbitrary",)),
    )(*args)


def _rwkv_scan_body(w_ref, kk_ref, b_ref, k_ref, r_ref, v_ref, s0_ref, y_ref, sT_ref, S, *dup, TB, NV, ns):
    tb = pl.program_id(1)

    @pl.when(tb == 0)
    def _():
        S[...] = s0_ref[...]

    srcs = (w_ref, kk_ref, b_ref, k_ref, r_ref)
    if ns > 1:
        for src, dst in zip(srcs, dup):
            x = src[...]
            dst[...] = jnp.concatenate([x] * ns, axis=-1)
        srcs = dup

    def step(t, carry):
        w, kk, bb, k, r = (ref[t] for ref in srcs)
        for vi in range(NV):
            s = S[vi]
            sa = jnp.sum(s * kk, axis=0, keepdims=True)
            vv = v_ref[t, pl.ds(vi, 1), :]
            s = s * w - sa * bb + vv * k
            S[vi] = s
            y_ref[t, pl.ds(vi, 1), :] = jnp.sum(s * r, axis=0, keepdims=True)
        return carry

    lax.fori_loop(0, TB, step, 0)

    @pl.when(tb == pl.num_programs(1) - 1)
    def _():
        sT_ref[...] = S[...]


def rwkv_scan(w, kk, b, k, r, v, s0, TB):
    T, NV, LN = v.shape
    ns = HD // NV
    assert T % TB == 0 and LN % LANES == 0 and w.shape[2] * ns == LN
    assert ns == 1 or LN == LANES
    kspec = pl.BlockSpec((TB, HD, LANES // ns), lambda c, t: (t, 0, c))
    vspec = pl.BlockSpec((TB, NV, LANES), lambda c, t: (t, 0, c))
    sspec = pl.BlockSpec((NV, HD, LANES), lambda c, t: (0, 0, c))
    return pl.pallas_call(
        functools.partial(_rwkv_scan_body, TB=TB, NV=NV, ns=ns),
        grid=(LN // LANES, T // TB),
        in_specs=[kspec] * 5 + [vspec, sspec],
        out_specs=[vspec, sspec],
        out_shape=[jax.ShapeDtypeStruct((T, NV, LN), F32), jax.ShapeDtypeStruct((NV, HD, LN), F32)],
        scratch_shapes=[pltpu.VMEM((NV, HD, LANES), F32)]
        + ([pltpu.VMEM((TB, HD, LANES), F32)] * 5 if ns > 1 else []),
        compiler_params=_cparams(("arbitrary", "arbitrary")),
    )(w, kk, b, k, r, v, s0)


def _rwkv_post_body(y_ref, r_ref, k_ref, v_ref, g_ref, vec_ref, bd_ref, o_ref):
    bd = bd_ref[...]
    y = y_ref[...]
    r_k, ln_w, ln_b = vec_ref[4:5, :], vec_ref[5:6, :], vec_ref[6:7, :]
    mean = _segsum(y, bd) * (1.0 / HD)
    d = y - mean
    var = _segsum(d * d, bd) * (1.0 / HD)
    yn = d * lax.rsqrt(var + RWKV_GN_EPS) * ln_w + ln_b
    v = v_ref[...]
    bonus = _segsum(r_ref[...] * k_ref[...] * r_k, bd) * v
    o_ref[...] = ((yn + bonus) * g_ref[...]).astype(o_ref.dtype)


def rwkv_post(grp, y, r, kmod, v, g, vecs8, bd):
    T, D = y.shape
    tm = grp.tm
    tile = pl.BlockSpec((tm, D), lambda i: (i, 0))
    return pl.pallas_call(
        _rwkv_post_body,
        grid=(T // tm,),
        in_specs=[tile] * 5 + [pl.BlockSpec((8, D), lambda i: (0, 0)),
                               pl.BlockSpec((MXU_DIM, MXU_DIM), lambda i: (0, 0))],
        out_specs=tile,
        out_shape=jax.ShapeDtypeStruct((T, D), BF16),
        compiler_params=_cparams(("arbitrary",)),
    )(y, r, kmod, v, g, vecs8, bd)


def _ssd_pre_body(*refs, per_token, tiles_per_seq):
    if per_token:
        x_ref, c0_ref, c1_ref, c2_ref, cw_ref, cb_ref, sm_ref, dtb_ref, xs_o, bc_o, dt_o = refs
        x = x_ref[...]
        taps = [c0_ref[...], c1_ref[...], c2_ref[...], x]
    else:
        x_ref, cw_ref, cb_ref, sm_ref, dtb_ref, xs_o, bc_o, dt_o, carry = refs
        x = x_ref[...]
        tm = x.shape[0]

        @pl.when((pl.program_id(0) % tiles_per_seq) == 0)
        def _():
            carry[...] = jnp.zeros_like(carry)

        ext = jnp.concatenate([carry[...], x], axis=0)
        taps = [pltpu.roll(ext, CONV_W - 1 - i, 0)[SUBLANES:, :] for i in range(CONV_W - 1)] + [x]
        carry[...] = x[tm - SUBLANES:, :]
    conv = cb_ref[...] + taps[0] * cw_ref[0:1, :]
    for i in range(1, CONV_W):
        conv = conv + taps[i] * cw_ref[i:i + 1, :]
    xc = _silu(conv)
    xs_o[...] = xc[:, :SSM_DIM]
    bc_o[...] = xc[:, SSM_DIM:]
    dt_o[...] = _softplus(sm_ref[...] + dtb_ref[...])


def ssd_pre(grp, xbc, conv0, cw8, cb, zsm_dt, dtb):
    T = xbc.shape[0]
    tm = grp.tm
    tile = lambda n: pl.BlockSpec((tm, n), lambda i: (i, 0))
    full = lambda shp: pl.BlockSpec(shp, lambda i: tuple(0 for _ in shp))
    w_specs = [full((8, CONV_DIM)), full((1, CONV_DIM)), tile(LANES), full((1, LANES))]
    if grp.per_token:
        in_specs = [tile(CONV_DIM)] * 4 + w_specs
        args = [xbc, conv0[:, 0], conv0[:, 1], conv0[:, 2], cw8, cb, zsm_dt, dtb]
        scratch = []
        tiles_per_seq = 1
    else:
        in_specs = [tile(CONV_DIM)] + w_specs
        args = [xbc, cw8, cb, zsm_dt, dtb]
        scratch = [pltpu.VMEM((SUBLANES, CONV_DIM), F32)]
        tiles_per_seq = grp.L // tm
    return pl.pallas_call(
        functools.partial(_ssd_pre_body, per_token=grp.per_token, tiles_per_seq=tiles_per_seq),
        grid=(T // tm,),
        in_specs=in_specs,
        out_specs=[tile(SSM_DIM), tile(CONV_DIM - SSM_DIM), tile(LANES)],
        out_shape=[jax.ShapeDtypeStruct((T, SSM_DIM), F32),
                   jax.ShapeDtypeStruct((T, CONV_DIM - SSM_DIM), F32),
                   jax.ShapeDtypeStruct((T, LANES), F32)],
        scratch_shapes=scratch,
        compiler_params=_cparams(("arbitrary",)),
    )(*args)


def _ssd_chunk_body(xs_ref, bc_ref, dtc_ref, dtr_ref, ac_ref, ar_ref, ex_ref, y_ref, sT_ref, St):
    Q = SSD_CHUNK
    c = pl.program_id(1)

    @pl.when(c == 0)
    def _():
        St[...] = jnp.zeros_like(St)

    ri = lax.broadcasted_iota(jnp.int32, (Q, Q), 0)
    ci = lax.broadcasted_iota(jnp.int32, (Q, Q), 1)
    causal = ri >= ci
    tri = jnp.where(causal, 1.0, 0.0).astype(BF16)
    triT = jnp.where(ri <= ci, 1.0, 0.0).astype(BF16)
    dtc = dtc_ref[...]
    a_col = dtc * (-jnp.exp(ac_ref[...]))
    a_row = dtr_ref[...] * (-jnp.exp(ar_ref[...]))
    acs_col = _dot01_lhs(tri, a_col)
    acs_row = _dot01(a_row, triT)
    ex = ex_ref[...]
    dt_full = _dot01(dtc, ex)
    acs_full = _dot01(acs_col, ex)
    xs = xs_ref[...]
    xdt = xs * dt_full
    xdec = xdt * jnp.exp(acs_full[Q - 1:Q, :] - acs_full)
    eacs = jnp.exp(acs_full)
    lane = lax.broadcasted_iota(jnp.int32, (Q, LANES), 1)
    row = lax.broadcasted_iota(jnp.int32, (LANES, LANES), 0)
    xdt_b = xdt.astype(BF16)
    for g in range(SSM_GROUPS):
        bm = bc_ref[:, g * SSM_STATE:(g + 1) * SSM_STATE].astype(BF16)
        cm = bc_ref[:, (SSM_GROUPS + g) * SSM_STATE:(SSM_GROUPS + g + 1) * SSM_STATE].astype(BF16)
        cb = _dot_nt(cm, bm)
        for jp in range(HEADS // SSM_GROUPS // 2):
            j = g * (HEADS // SSM_GROUPS // 2) + jp
            sl = slice(j * LANES, (j + 1) * LANES)
            outs = []
            for hh in range(2):
                h = 2 * j + hh
                diff = acs_col[:, h:h + 1] - acs_row[h:h + 1, :]
                lm = jnp.where(causal, jnp.exp(jnp.minimum(diff, 0.0)), 0.0)
                outs.append(_dot((cb * lm).astype(BF16), xdt_b[:, sl]))
            y_diag = jnp.where(lane < HD, outs[0], outs[1])
            s_old = St[j]
            y_off = _dot_nt(cm, s_old.astype(BF16)) * eacs[:, sl]
            y_ref[:, sl] = y_diag + y_off
            new = _dot(jnp.transpose(xdec[:, sl]).astype(BF16), bm)
            tot0 = jnp.exp(acs_row[2 * j:2 * j + 1, Q - 1:Q])
            tot1 = jnp.exp(acs_row[2 * j + 1:2 * j + 2, Q - 1:Q])
            St[j] = s_old * jnp.where(row < HD, tot0, tot1) + new

    @pl.when(c == pl.num_programs(1) - 1)
    def _():
        sT_ref[...] = St[...]


def _dot01_lhs(m01, x):
    a, b, c = _split3(x)
    return _dot(m01, a) + _dot(m01, b) + _dot(m01, c)


def ssd_chunk(B, L, xs, bc, dtp, dtpT, alog_col, alog_row, expand):
    Q = SSD_CHUNK
    nc = L // Q
    return pl.pallas_call(
        _ssd_chunk_body,
        grid=(B, nc),
        in_specs=[pl.BlockSpec((Q, SSM_DIM), lambda b, c: (b * nc + c, 0)),
                  pl.BlockSpec((Q, CONV_DIM - SSM_DIM), lambda b, c: (b * nc + c, 0)),
                  pl.BlockSpec((Q, LANES), lambda b, c: (b * nc + c, 0)),
                  pl.BlockSpec((None, LANES, Q), lambda b, c: (b, 0, c)),
                  pl.BlockSpec((1, LANES), lambda b, c: (0, 0)),
                  pl.BlockSpec((LANES, 1), lambda b, c: (0, 0)),
                  pl.BlockSpec((LANES, SSM_DIM), lambda b, c: (0, 0))],
        out_specs=[pl.BlockSpec((Q, SSM_DIM), lambda b, c: (b * nc + c, 0)),
                   pl.BlockSpec((None, HEADS // 2, LANES, SSM_STATE), lambda b, c: (b, 0, 0, 0))],
        out_shape=[jax.ShapeDtypeStruct((B * L, SSM_DIM), F32),
                   jax.ShapeDtypeStruct((B, HEADS // 2, LANES, SSM_STATE), F32)],
        scratch_shapes=[pltpu.VMEM((HEADS // 2, LANES, SSM_STATE), F32)],
        compiler_params=_cparams(("arbitrary", "arbitrary")),
    )(xs, bc, dtp, dtpT, alog_row, alog_col, expand)


def _ssd_step_body(h_ref, x_ref, b_ref, c_ref, dt_ref, al_ref, y_ref, hn_ref):
    dt = dt_ref[...]
    dA = jnp.exp(dt * (-jnp.exp(al_ref[...])))
    bm, cm = b_ref[...], c_ref[...]
    for p in range(HD):
        hp = h_ref[p] * dA + (x_ref[p:p + 1, :] * dt) * bm
        hn_ref[p] = hp
        y_ref[p:p + 1, :] = jnp.sum(hp * cm, axis=0, keepdims=True)


def ssd_step(h0, x, bm, cm, dt, alog):
    LN = h0.shape[-1]
    lt = lambda n: pl.BlockSpec((n, LANES), lambda c: (0, c))
    hs = pl.BlockSpec((HD, SSM_STATE, LANES), lambda c: (0, 0, c))
    return pl.pallas_call(
        _ssd_step_body,
        grid=(LN // LANES,),
        in_specs=[hs, lt(HD), lt(SSM_STATE), lt(SSM_STATE), lt(1), lt(1)],
        out_specs=[lt(HD), hs],
        out_shape=[jax.ShapeDtypeStruct((HD, LN), F32), jax.ShapeDtypeStruct((HD, SSM_STATE, LN), F32)],
        compiler_params=_cparams(("arbitrary",)),
    )(h0, x, bm, cm, dt, alog)


def _ssd_post_body(y_ref, xs_ref, zg_ref, d_ref, nw_ref, o_ref):
    y = (y_ref[...] + d_ref[...] * xs_ref[...]) * _silu(zg_ref[...])
    gs = SSM_DIM // SSM_GROUPS
    for g in range(SSM_GROUPS):
        sl = slice(g * gs, (g + 1) * gs)
        yg = y[:, sl]
        n = yg * lax.rsqrt(jnp.mean(yg * yg, axis=-1, keepdims=True) + NORM_EPS)
        o_ref[:, sl] = (n * nw_ref[:, sl]).astype(o_ref.dtype)


def ssd_post(grp, y, xs, zg, dfull, nw):
    T, D = y.shape
    tm = grp.tm
    tile = pl.BlockSpec((tm, D), lambda i: (i, 0))
    row = pl.BlockSpec((1, D), lambda i: (0, 0))
    return pl.pallas_call(
        _ssd_post_body,
        grid=(T // tm,),
        in_specs=[tile, tile, tile, row, row],
        out_specs=tile,
        out_shape=jax.ShapeDtypeStruct((T, D), BF16),
        compiler_params=_cparams(("arbitrary",)),
    )(y, xs, zg, dfull, nw)


def _mla_prep_body(zq_ref, zsm_ref, cs_ref, qn_ref, kvn_ref, wq_ref, wkv_ref, gq_ref, gk_ref,
                   q_o, k_o, v_o, lat_o, kpe_o, wq_b, wkv_b, *, v_transposed):
    @pl.when(pl.program_id(0) == 0)
    def _():
        wq_b[...] = wq_ref[...].astype(BF16)
        wkv_b[...] = wkv_ref[...].astype(BF16)

    HP = HEADS * LANES
    zq = zq_ref[...]
    qd, kvd = zq[:, :Q_LORA], zq[:, Q_LORA:]
    qn = qd * lax.rsqrt(jnp.mean(qd * qd, axis=-1, keepdims=True) + NORM_EPS) * qn_ref[...]
    lat = kvd * lax.rsqrt(jnp.mean(kvd * kvd, axis=-1, keepdims=True) + NORM_EPS) * kvn_ref[...]
    lat_o[...] = lat
    cosv, sinv = cs_ref[:, :LANES], cs_ref[:, LANES:]
    kpe = zsm_ref[:, :LANES] * cosv + zsm_ref[:, LANES:2 * LANES] * sinv
    kpe_o[...] = kpe
    qq = _dot(qn.astype(BF16), wq_b[...])
    kv = _dot(lat.astype(BF16), wkv_b[...])
    if v_transposed:
        for j in range(HEADS // 2):
            v_o[j] = jnp.transpose(kv[:, HP + j * LANES:HP + (j + 1) * LANES]).astype(v_o.dtype)
    else:
        v_o[...] = kv[:, HP:].astype(v_o.dtype)
    gq, gk = gq_ref[...], gk_ref[...]
    nope_one = jnp.where(lax.broadcasted_iota(jnp.int32, (1, LANES), 1) < NOPE_DIM, 1.0, 0.0)
    cq = cosv + nope_one
    for h in range(HEADS):
        sl = slice(h * LANES, (h + 1) * LANES)
        qh = qq[:, sl] * cq + qq[:, HP + h * LANES:HP + (h + 1) * LANES] * sinv
        rq = lax.rsqrt(jnp.sum(qh * qh, axis=-1, keepdims=True) * (1.0 / QK_DIM) + NORM_EPS)
        q_o[:, sl] = (qh * rq * gq).astype(q_o.dtype)
        kh = kv[:, sl] + kpe
        rk = lax.rsqrt(jnp.sum(kh * kh, axis=-1, keepdims=True) * (1.0 / QK_DIM) + NORM_EPS)
        k_o[:, sl] = (kh * rk * gk).astype(k_o.dtype)


def mla_prep(grp, zq, zsm_kpe, cs, qn, kvn, wq2, wkv, gq, gk):
    T = zq.shape[0]
    tm = grp.tm
    HP = HEADS * LANES
    tile = lambda n: pl.BlockSpec((tm, n), lambda i: (i, 0))
    full = lambda shp: pl.BlockSpec(shp, lambda i: tuple(0 for _ in shp))
    if grp.per_token:
        cs_spec = full((1, 2 * LANES))
        v_spec = tile(HEADS * HD)
        v_shape = jax.ShapeDtypeStruct((T, HEADS * HD), BF16)
    else:
        tps = grp.L // tm
        cs_spec = pl.BlockSpec((tm, 2 * LANES), lambda i: (i % tps, 0))
        v_spec = pl.BlockSpec((None, HEADS // 2, None, LANES, tm), lambda i: (i // tps, 0, i % tps, 0, 0))
        v_shape = jax.ShapeDtypeStruct((grp.B, HEADS // 2, tps, LANES, tm), BF16)
    return pl.pallas_call(
        functools.partial(_mla_prep_body, v_transposed=not grp.per_token),
        grid=(T // tm,),
        in_specs=[tile(Q_LORA + KV_LORA), tile(2 * LANES), cs_spec, full((1, Q_LORA)), full((1, KV_LORA)),
                  full((Q_LORA, 2 * HP)), full((KV_LORA, HP + HEADS * HD)), full((1, LANES)), full((1, LANES))],
        out_specs=[tile(HP), tile(HP), v_spec, tile(KV_LORA), tile(LANES)],
        out_shape=[jax.ShapeDtypeStruct((T, HP), BF16), jax.ShapeDtypeStruct((T, HP), BF16),
                   v_shape, jax.ShapeDtypeStruct((T, KV_LORA), F32),
                   jax.ShapeDtypeStruct((T, LANES), F32)],
        scratch_shapes=[pltpu.VMEM((Q_LORA, 2 * HP), BF16), pltpu.VMEM((KV_LORA, HP + HEADS * HD), BF16)],
        compiler_params=_cparams(("arbitrary",)),
    )(zq, zsm_kpe, cs, qn, kvn, wq2, wkv, gq, gk)


def _flash_body(q_ref, k_ref, vt_ref, o_ref, *, tq):
    qi = pl.program_id(2)
    ri = lax.broadcasted_iota(jnp.int32, (tq, tq), 0)
    ci = lax.broadcasted_iota(jnp.int32, (tq, tq), 1)
    NH = FLASH_HEADS
    qs = [q_ref[:, hh * LANES:(hh + 1) * LANES] for hh in range(NH)]

    def block(ki, carry, masked):
        off = pl.multiple_of(ki * tq, tq)
        kb = k_ref[pl.ds(off, tq), :]
        out = []
        for hh in range(NH):
            m, l, acc = carry[hh]
            st = _dot_nt(kb[:, hh * LANES:(hh + 1) * LANES], qs[hh])
            if masked:
                st = jnp.where(ri <= ci, st, NEG_INF)
            m_new = jnp.maximum(m, jnp.max(st, axis=0, keepdims=True))
            corr = jnp.exp(m - m_new)
            pt = jnp.exp(st - m_new)
            l = l * corr + jnp.sum(pt, axis=0, keepdims=True)
            vt = vt_ref[hh // 2, ki, (hh % 2) * HD:(hh % 2 + 1) * HD, :]
            acc = acc * corr + _dot(vt, pt.astype(BF16))
            out.append((m_new, l, acc))
        return tuple(out)

    init = tuple((jnp.full((1, tq), NEG_INF, F32), jnp.zeros((1, tq), F32), jnp.zeros((HD, tq), F32))
                 for _ in range(NH))
    carry = lax.fori_loop(0, qi, lambda ki, c: block(ki, c, False), init)
    carry = block(qi, carry, True)
    ot = jnp.concatenate([c[2] / c[1] for c in carry], axis=0)
    o_ref[...] = jnp.transpose(ot).astype(o_ref.dtype)


def flash_causal(B, L, q, k, vt, tq):
    nq = L // tq
    NH = FLASH_HEADS
    PW = NH * LANES
    assert vt.shape == (B, HEADS // 2, nq, LANES, tq)
    return pl.pallas_call(
        functools.partial(_flash_body, tq=tq),
        grid=(B, HEADS // NH, nq),
        in_specs=[pl.BlockSpec((tq, PW), lambda b, hg, i: (b * nq + i, hg)),
                  pl.BlockSpec((L, PW), lambda b, hg, i: (b, hg)),
                  pl.BlockSpec((None, NH // 2, nq, LANES, tq), lambda b, hg, i: (b, hg, 0, 0, 0))],
        out_specs=pl.BlockSpec((tq, NH * HD), lambda b, hg, i: (b * nq + i, hg)),
        out_shape=jax.ShapeDtypeStruct((B * L, HEADS * HD), BF16),
        compiler_params=_cparams(("arbitrary", "arbitrary", "arbitrary")),
    )(q, k, vt)


def _absorb_body(q_ref, wkv_ref, gk_ref, pm_ref, ql_o, qr_o):
    q = q_ref[...].astype(F32) * gk_ref[...]
    lane = lax.broadcasted_iota(jnp.int32, q.shape, 1)
    qn = jnp.where(lane < NOPE_DIM, q, 0.0).astype(BF16)
    ql_o[...] = _dot_nt(qn, wkv_ref[...].astype(BF16)).astype(ql_o.dtype)
    qr_o[...] = _dot(q.astype(BF16), pm_ref[...]).astype(qr_o.dtype)


def mla_absorb(qpad, wkv2d, gk, pm):
    Bd = qpad.shape[0]
    return pl.pallas_call(
        _absorb_body,
        grid=(HEADS,),
        in_specs=[pl.BlockSpec((Bd, LANES), lambda h: (0, h)),
                  pl.BlockSpec((KV_LORA, LANES), lambda h: (0, h)),
                  pl.BlockSpec((1, LANES), lambda h: (0, 0)),
                  pl.BlockSpec((LANES, ROPE_DIM), lambda h: (0, 0))],
        out_specs=[pl.BlockSpec((None, Bd, KV_LORA), lambda h: (h, 0, 0)),
                   pl.BlockSpec((None, Bd, ROPE_DIM), lambda h: (h, 0, 0))],
        out_shape=[jax.ShapeDtypeStruct((HEADS, Bd, KV_LORA), BF16),
                   jax.ShapeDtypeStruct((HEADS, Bd, ROPE_DIM), BF16)],
        compiler_params=_cparams(("arbitrary",)),
    )(qpad, wkv2d, gk, pm)


def _decode_body(pt_ref, *refs, G):
    lat_refs = refs[:G]
    kpe_refs = refs[G:2 * G]
    ql_ref, qr_ref, wkt_ref, nlat_ref, nkpe_ref, o_ref, lhs, s_s, lat_s, m_s, l_s, acc_s = refs[2 * G:]
    g = pl.program_id(1)
    NK = HEADS * NOPE_DIM
    KEYS = G * PAGE
    BLK = 2 * PAGE

    @pl.when(jnp.logical_and(pl.program_id(0) == 0, g == 0))
    def _():
        lhs[0:NK, :] = wkt_ref[...].astype(BF16)

    @pl.when(g == 0)
    def _():
        lhs[NK:NK + HEADS, :] = ql_ref[...]
        m_s[...] = jnp.full_like(m_s, NEG_INF)
        l_s[...] = jnp.zeros_like(l_s)
        acc_s[...] = jnp.zeros_like(acc_s)
        s_s[:, KEYS:] = jnp.full((HEADS, PAGE), NEG_INF, F32)
        lat_s[KEYS:, :] = jnp.zeros((PAGE, KV_LORA), BF16)

    qr = qr_ref[...]

    def scores(lat_b, kpe_t):
        big = _dot_nt(lhs[...], lat_b)
        kn = big[:NK]
        ssq = jnp.sum((kn * kn).reshape(NOPE_DIM, HEADS, kn.shape[1]), axis=0)
        ssq = ssq + jnp.sum(kpe_t * kpe_t, axis=0, keepdims=True)
        raw = big[NK:] + _dot(qr, kpe_t.astype(BF16))
        return raw * lax.rsqrt(ssq * (1.0 / QK_DIM) + NORM_EPS)

    for j in range(0, G, 2):
        lat_b = jnp.concatenate([lat_refs[j][...], lat_refs[j + 1][...]], axis=0).astype(BF16)
        kpe_t = jnp.concatenate([kpe_refs[j][...], kpe_refs[j + 1][...]], axis=1)
        lat_s[j * PAGE:j * PAGE + BLK, :] = lat_b
        s_s[:, j * PAGE:j * PAGE + BLK] = scores(lat_b, kpe_t)

    @pl.when(g == pl.num_programs(1) - 1)
    def _():
        lat_b = nlat_ref[...].astype(BF16)
        col = lax.broadcasted_iota(jnp.int32, (HEADS, PAGE), 1)
        lat_s[KEYS:, :] = lat_b
        s_s[:, KEYS:] = jnp.where(col == 0, scores(lat_b, nkpe_ref[...]), NEG_INF)

    s = s_s[...]
    m = m_s[...]
    m_new = jnp.maximum(m, jnp.max(s, axis=-1, keepdims=True))
    corr = jnp.exp(m - m_new)
    p = jnp.exp(s - m_new)
    l_new = l_s[...] * corr + jnp.sum(p, axis=-1, keepdims=True)
    acc = acc_s[...] * corr + _dot(p.astype(BF16), lat_s[...])
    l_s[...] = l_new
    acc_s[...] = acc
    m_s[...] = m_new

    @pl.when(g == pl.num_programs(1) - 1)
    def _():
        o_ref[...] = acc / l_new


def mla_decode(l, cache_lat, cache_kpe_t, page_table, ql, qr, wk_t, nlat, nkpe_t, G):
    Bd, n_pages = page_table.shape
    assert n_pages % G == 0 and G % 2 == 0
    NK = HEADS * NOPE_DIM
    lat_specs = [pl.BlockSpec((None, None, PAGE, KV_LORA),
                              functools.partial(lambda b, g, pt, j: (l, pt[b, g * G + j], 0, 0), j=j))
                 for j in range(G)]
    kpe_specs = [pl.BlockSpec((None, None, ROPE_DIM, PAGE),
                              functools.partial(lambda b, g, pt, j: (l, pt[b, g * G + j], 0, 0), j=j))
                 for j in range(G)]
    bspec = lambda r, c: pl.BlockSpec((None, r, c), lambda b, g, pt: (b, 0, 0))
    grid_spec = pltpu.PrefetchScalarGridSpec(
        num_scalar_prefetch=1,
        grid=(Bd, n_pages // G),
        in_specs=lat_specs + kpe_specs + [
            bspec(HEADS, KV_LORA), bspec(HEADS, ROPE_DIM),
            pl.BlockSpec((NK, KV_LORA), lambda b, g, pt: (0, 0)),
            bspec(PAGE, KV_LORA), bspec(ROPE_DIM, PAGE)],
        out_specs=bspec(HEADS, KV_LORA),
        scratch_shapes=[pltpu.VMEM((NK + HEADS, KV_LORA), BF16),
                        pltpu.VMEM((HEADS, (G + 1) * PAGE), F32),
                        pltpu.VMEM(((G + 1) * PAGE, KV_LORA), BF16),
                        pltpu.VMEM((HEADS, 1), F32), pltpu.VMEM((HEADS, 1), F32),
                        pltpu.VMEM((HEADS, KV_LORA), F32)],
    )
    return pl.pallas_call(
        functools.partial(_decode_body, G=G),
        grid_spec=grid_spec,
        out_shape=jax.ShapeDtypeStruct((Bd, HEADS, KV_LORA), F32),
        compiler_params=_cparams(("arbitrary", "arbitrary")),
    )(page_table, *([cache_lat] * G), *([cache_kpe_t] * G), ql, qr, wk_t, nlat, nkpe_t)


def _vup_body(a_ref, w_ref, o_ref):
    w = w_ref[...].astype(BF16)
    lane = lax.broadcasted_iota(jnp.int32, o_ref.shape, 1)
    o0 = _dot(a_ref[0].astype(BF16), w[:, :LANES])
    o1 = _dot(a_ref[1].astype(BF16), w[:, LANES:])
    o0 = pltpu.roll(o0, HD, 1)
    o_ref[...] = jnp.where(lane < HD, o0, o1).astype(o_ref.dtype)


def mla_vup(acc_h, wkv2d):
    Bd = acc_h.shape[1]
    return pl.pallas_call(
        _vup_body,
        grid=(HEADS // 2,),
        in_specs=[pl.BlockSpec((2, Bd, KV_LORA), lambda j: (j, 0, 0)),
                  pl.BlockSpec((KV_LORA, 2 * LANES), lambda j: (0, j))],
        out_specs=pl.BlockSpec((Bd, LANES), lambda j: (0, j)),
        out_shape=jax.ShapeDtypeStruct((Bd, HEADS * HD), BF16),
        compiler_params=_cparams(("arbitrary",)),
    )(acc_h, wkv2d)


def _merge_body(ya_ref, yb_ref, yc_ref, w_ref, g0_ref, g1_ref, g2_ref, o_ref, wb):
    @pl.when(pl.program_id(1) == 0)
    def _():
        wb[...] = w_ref[...].astype(BF16)

    acc = g0_ref[...].astype(F32) * _dot(ya_ref[...], wb[0])
    acc = acc + g1_ref[...].astype(F32) * _dot(yb_ref[...], wb[1])
    acc = acc + g2_ref[...].astype(F32) * _dot(yc_ref[...], wb[2])
    o_ref[...] = acc.astype(o_ref.dtype)


def merge_mm(l, ya, yb, yc, wb, gates, tm, tn):
    T, K = ya.shape
    D = wb.shape[-1]
    yspec = pl.BlockSpec((tm, K), lambda j, i: (i, 0))
    return pl.pallas_call(
        _merge_body,
        grid=(D // tn, T // tm),
        in_specs=[yspec, yspec, yspec,
                  pl.BlockSpec((None, N_BRANCH, K, tn), lambda j, i: (l, 0, 0, j))
                  ] + [pl.BlockSpec((tm, tn), functools.partial(lambda j, i, b: (i, b * (D // tn) + j), b=b))
                       for b in range(N_BRANCH)],
        out_specs=pl.BlockSpec((tm, tn), lambda j, i: (i, j)),
        out_shape=jax.ShapeDtypeStruct((T, D), BF16),
        scratch_shapes=[pltpu.VMEM((N_BRANCH, K, tn), BF16)],
        compiler_params=_cparams(("arbitrary", "arbitrary")),
    )(ya, yb, yc, wb, gates, gates, gates)


def _lane_partner(x, sh, lane):
    up = pltpu.roll(x, LANES - sh, 1)
    dn = pltpu.roll(x, sh, 1)
    return jnp.where((lane & sh) == 0, up, dn)


def _seg8(x, op, lane):
    for sh in (1, 2, 4):
        x = op(x, _lane_partner(x, sh, lane))
    return x


def _route_body(lg_ref, bias_ref, gate_o, idx_o):
    lg = lg_ref[...]
    shape = lg.shape
    lane = lax.broadcasted_iota(jnp.int32, shape, 1)
    valid = lane < N_EXPERTS
    s = _sigmoid(lg)
    NINF = -jnp.inf
    sb = jnp.where(valid, s + bias_ref[...], NINF)
    BIG = jnp.int32(1 << 20)
    m1 = _seg8(sb, jnp.maximum, lane)
    i1 = _seg8(jnp.where(sb == m1, lane, BIG), jnp.minimum, lane)
    m2 = _seg8(jnp.where(lane == i1, NINF, sb), jnp.maximum, lane)
    gs = jnp.where(valid, m1 + m2, NINF)
    grp = lane >> 3
    gsel = jnp.zeros(shape, jnp.bool_)
    for _ in range(TOPK_GROUPS):
        mx = jnp.max(gs, axis=-1, keepdims=True)
        first = jnp.min(jnp.where(gs == mx, lane, BIG), axis=-1, keepdims=True)
        hit = grp == (first >> 3)
        gsel = jnp.logical_or(gsel, hit)
        gs = jnp.where(hit, NINF, gs)
    cand = jnp.where(valid, jnp.where(gsel, sb, NEG_INF), NINF)
    sel = jnp.zeros(shape, jnp.bool_)
    idx = jnp.zeros(shape, jnp.int32)
    for it in range(TOP_K):
        mx = jnp.max(cand, axis=-1, keepdims=True)
        first = jnp.min(jnp.where(cand == mx, lane, BIG), axis=-1, keepdims=True)
        hit = lane == first
        sel = jnp.logical_or(sel, hit)
        cand = jnp.where(hit, NINF, cand)
        idx = jnp.where(lane == it, first, idx)
    w = jnp.where(sel, s, 0.0)
    gate_o[...] = w / jnp.sum(w, axis=-1, keepdims=True) * ROUTED_SCALE
    idx_o[...] = idx


def route(logits, bias_pad, tm):
    T = logits.shape[0]
    tile = pl.BlockSpec((tm, LANES), lambda i: (i, 0))
    return pl.pallas_call(
        _route_body,
        grid=(T // tm,),
        in_specs=[tile, pl.BlockSpec((1, LANES), lambda i: (0, 0))],
        out_specs=[tile, tile],
        out_shape=[jax.ShapeDtypeStruct((T, LANES), F32), jax.ShapeDtypeStruct((T, LANES), jnp.int32)],
        compiler_params=_cparams(("arbitrary",)),
    )(logits, bias_pad)


def _expert_body(te_ref, nu_ref, x_ref, wg_ref, wu_ref, wd_ref, rw_ref, o_ref, wg_b, wu_b, wd_b):
    i = pl.program_id(0)
    prev = te_ref[jnp.maximum(i - 1, 0)]
    used = i < nu_ref[0]

    @pl.when(jnp.logical_and(used, jnp.logical_or(i == 0, te_ref[i] != prev)))
    def _():
        wg_b[...] = wg_ref[...].astype(BF16)
        wu_b[...] = wu_ref[...].astype(BF16)
        wd_b[...] = wd_ref[...].astype(BF16)

    @pl.when(used)
    def _():
        x = x_ref[...]
        hid = _silu(_dot(x, wg_b[...])) * _dot(x, wu_b[...])
        y = _dot(hid.astype(BF16), wd_b[...])
        o_ref[...] = (y * rw_ref[...]).astype(o_ref.dtype)

    @pl.when(jnp.logical_not(used))
    def _():
        o_ref[...] = jnp.zeros_like(o_ref)


def expert_ffn(l, tile_expert, n_used, xs, w_gate, w_up, w_down, row_w, tm):
    M, D = xs.shape
    FF = w_gate.shape[-1]
    grid_spec = pltpu.PrefetchScalarGridSpec(
        num_scalar_prefetch=2,
        grid=(M // tm,),
        in_specs=[pl.BlockSpec((tm, D), lambda i, te, nu: (i, 0)),
                  pl.BlockSpec((None, None, D, FF), lambda i, te, nu: (l, te[i], 0, 0)),
                  pl.BlockSpec((None, None, D, FF), lambda i, te, nu: (l, te[i], 0, 0)),
                  pl.BlockSpec((None, None, FF, D), lambda i, te, nu: (l, te[i], 0, 0)),
                  pl.BlockSpec((tm, 1), lambda i, te, nu: (i, 0))],
        out_specs=pl.BlockSpec((tm, D), lambda i, te, nu: (i, 0)),
        scratch_shapes=[pltpu.VMEM((D, FF), BF16), pltpu.VMEM((D, FF), BF16), pltpu.VMEM((FF, D), BF16)],
    )
    return pl.pallas_call(
        _expert_body,
        grid_spec=grid_spec,
        out_shape=jax.ShapeDtypeStruct((M, D), BF16),
        compiler_params=_cparams(("arbitrary",)),
    )(tile_expert, n_used, xs, w_gate, w_up, w_down, row_w)


def _glu_body(x_ref, wg_ref, wu_ref, o_ref, wg_b, wu_b):
    @pl.when(pl.program_id(0) == 0)
    def _():
        wg_b[...] = wg_ref[...].astype(BF16)
        wu_b[...] = wu_ref[...].astype(BF16)

    x = x_ref[...]
    o_ref[...] = (_silu(_dot(x, wg_b[...])) * _dot(x, wu_b[...])).astype(o_ref.dtype)


def glu_mm(l, x, wg, wu, tm):
    T, D = x.shape
    FF = wg.shape[-1]
    wspec = pl.BlockSpec((None, D, FF), lambda i: (l, 0, 0))
    return pl.pallas_call(
        _glu_body,
        grid=(T // tm,),
        in_specs=[pl.BlockSpec((tm, D), lambda i: (i, 0)), wspec, wspec],
        out_specs=pl.BlockSpec((tm, FF), lambda i: (i, 0)),
        out_shape=jax.ShapeDtypeStruct((T, FF), BF16),
        scratch_shapes=[pltpu.VMEM((D, FF), BF16), pltpu.VMEM((D, FF), BF16)],
        compiler_params=_cparams(("arbitrary",)),
    )(x, wg, wu)


def _rope_tables(pos):
    inv = 1.0 / (ROPE_THETA ** (jnp.arange(0, ROPE_DIM, 2, dtype=F32) / ROPE_DIM))
    ang = pos.astype(F32)[:, None] * inv[None, :]
    ang = jnp.concatenate([ang, ang], axis=-1)
    cos, sin = jnp.cos(ang), jnp.sin(ang)
    n = pos.shape[0]
    z64 = jnp.zeros((n, NOPE_DIM), F32)
    z32 = jnp.zeros((n, LANES - QK_DIM), F32)
    return jnp.concatenate([z64, cos, z32, z64, sin, z32], axis=-1)


def _rotate_half_cols(w):
    half = ROPE_DIM // 2
    return jnp.concatenate([-w[..., half:], w[..., :half]], axis=-1)


def _qk_gain_pad(g):
    full = jnp.concatenate([g, g[NOPE_DIM:]], axis=-1)
    return jnp.pad(full, (0, LANES - QK_DIM)).reshape(1, LANES)


def _layer_weights(l, W):
    D = D_MODEL
    o = {}
    w_in = W["w_in"][l]
    c0 = 0
    o["wA"] = w_in[:, c0:c0 + RWKV_IN]; c0 += RWKV_IN
    o["wZ"] = w_in[:, c0:c0 + SSM_DIM]; c0 += SSM_DIM
    o["wX"] = w_in[:, c0:c0 + CONV_DIM]; c0 += CONV_DIM
    w_dt = w_in[:, c0:c0 + HEADS]; c0 += HEADS
    o["wQKV"] = w_in[:, c0:c0 + Q_LORA + KV_LORA]; c0 += Q_LORA + KV_LORA
    w_kpe = w_in[:, c0:c0 + ROPE_DIM]; c0 += ROPE_DIM
    o["wG"] = w_in[:, c0:]
    zpad = lambda n: jnp.zeros((D, n), F32)
    o["wS"] = jnp.concatenate([zpad(NOPE_DIM), w_kpe, zpad(LANES - QK_DIM),
                               zpad(NOPE_DIM), _rotate_half_cols(w_kpe), zpad(LANES - QK_DIM),
                               w_dt, zpad(LANES - HEADS)], axis=1)
    o["mu"] = W["rwkv_mu"][l].reshape(1, RWKV_IN)
    o["vecs8"] = jnp.pad(W["rwkv_vecs"][l], ((0, 1), (0, 0)))
    zl = jnp.zeros((W_LORA, RWKV_DIM), F32)
    o["wup"] = jnp.concatenate([W["rwkv_w_up"][l], zl], axis=0).astype(BF16)
    o["aup"] = jnp.concatenate([zl, W["rwkv_a_up"][l]], axis=0).astype(BF16)
    o["gup"] = W["rwkv_g_up"][l].astype(BF16)
    if l > 0:
        o["v0"] = W["rwkv_v0"][l - 1].reshape(1, RWKV_DIM)
        o["vdn"] = jnp.pad(W["rwkv_v_down"][l - 1], ((0, 0), (0, LANES - V_LORA))).astype(BF16)
        o["vup"] = jnp.pad(W["rwkv_v_up"][l - 1], ((0, LANES - V_LORA), (0, 0))).astype(BF16)
    o["cw8"] = jnp.pad(W["ssm_conv_w"][l], ((0, 8 - CONV_W), (0, 0)))
    o["cb"] = W["ssm_conv_b"][l].reshape(1, CONV_DIM)
    o["dtb"] = jnp.pad(W["ssm_dt_bias"][l], (0, LANES - HEADS)).reshape(1, LANES)
    alog = jnp.pad(W["ssm_a_log"][l], (0, LANES - HEADS))
    o["alog_row"] = alog.reshape(1, LANES)
    o["alog_col"] = alog.reshape(LANES, 1)
    o["dfull"] = jnp.repeat(W["ssm_d"][l], HD).reshape(1, SSM_DIM)
    o["nw"] = W["ssm_norm_w"][l].reshape(1, SSM_DIM)
    wq = W["mla_q_up"][l]
    padh = lambda a: jnp.pad(a, ((0, 0), (0, 0), (0, LANES - a.shape[-1])))
    wq_rot = jnp.concatenate([jnp.zeros_like(wq[..., :NOPE_DIM]), _rotate_half_cols(wq[..., NOPE_DIM:])], axis=-1)
    o["wq2"] = jnp.concatenate([padh(wq).reshape(Q_LORA, -1), padh(wq_rot).reshape(Q_LORA, -1)], axis=1)
    wkv = W["mla_kv_up"][l]
    o["wkv_prep"] = jnp.concatenate([padh(wkv[..., :NOPE_DIM]).reshape(KV_LORA, -1),
                                     wkv[..., NOPE_DIM:].reshape(KV_LORA, -1)], axis=1)
    o["wkv2d"] = wkv.reshape(KV_LORA, HEADS * 2 * HD)
    o["wk_t"] = jnp.transpose(wkv[..., :NOPE_DIM], (2, 1, 0)).reshape(HEADS * NOPE_DIM, KV_LORA)
    o["qn"] = W["mla_q_norm"][l].reshape(1, Q_LORA)
    o["kvn"] = W["mla_kv_norm"][l].reshape(1, KV_LORA)
    o["gq"] = _qk_gain_pad(W["mla_qk_gain_q"][l]) * ATTN_SCALE
    o["gk"] = _qk_gain_pad(W["mla_qk_gain_k"][l])
    o["w_router"] = jnp.pad(W["w_router"][l], ((0, 0), (0, LANES - N_EXPERTS)))
    o["router_bias"] = jnp.pad(W["router_bias"][l], (0, LANES - N_EXPERTS)).reshape(1, LANES)
    return o


def _chain_layout(a, B, L):
    a = a.reshape(B, L, HEADS, HD)
    return jnp.transpose(a, (1, 3, 0, 2)).reshape(L, HD, B * HEADS)


def _mix_block(grp, l, x, mods, st, lw, W, consts, v_first, attend):
    B, L, T, tm = grp.B, grp.L, grp.T, grp.tm
    D = D_MODEL
    sh1, sc1, g1 = mods[0], mods[1], mods[2]
    h = norm_mod(grp, x, W["norm_mix"][l], sc1, sh1)
    zA = mm(h, lw["wA"], tm, RWKV_IN // 2)
    zg = mm(h, lw["wZ"], tm, SSM_DIM)
    xbc = mm(h, lw["wX"], tm, CONV_DIM // 2)
    zq = mm(h, lw["wQKV"], tm, Q_LORA + KV_LORA)
    zsm = mm(h, lw["wS"], tm, 3 * LANES)
    gates = mm(h, lw["wG"], tm, D // 2, epi="sigmoid")

    shift0, wkv0, conv0, ssm0 = st
    if grp.per_token:
        prev = shift0
        shift_new = zA
    else:
        prev = jnp.zeros((B, 1, RWKV_IN), F32)
        shift_new = zA.reshape(B, L, RWKV_IN)[:, -1]
    l1 = None if l == 0 else (lw["v0"], lw["vdn"], lw["vup"], v_first)
    r, dec, kmod, v, kk, bb, g = rwkv_prep(grp, zA, prev, lw["mu"], lw["vecs8"], lw["wup"], lw["aup"],
                                           lw["gup"], consts["bd64"], l1)
    if l == 0:
        v_first = v
    nchain = B * HEADS
    ns = max(1, LANES // nchain)
    NV = HD // ns
    lay = lambda a: _chain_layout(a, B, L)
    vv = jnp.transpose(v.reshape(B, L, HEADS, ns, NV), (1, 4, 3, 0, 2)).reshape(L, NV, ns * nchain)
    if grp.per_token:
        s0 = jnp.transpose(wkv0.reshape(nchain, HD, HD), (1, 2, 0))
    else:
        s0 = jnp.zeros((NV, HD, ns * nchain), F32)
    TB = 1 if L == 1 else min(L, SCAN_TB)
    y, sT = rwkv_scan(lay(dec), lay(kk), lay(bb), lay(kmod), lay(r), vv, s0, TB)
    y = jnp.transpose(y.reshape(L, NV, ns, B, HEADS), (3, 0, 4, 2, 1)).reshape(T, RWKV_DIM)
    wkv_new = jnp.transpose(sT.reshape(NV, HD, ns, B, HEADS), (3, 4, 2, 0, 1)).reshape(B, HEADS, HD, HD)
    ya = rwkv_post(grp, y, r, kmod, v, g, lw["vecs8"], consts["bd64"])

    xs, bc, dtp = ssd_pre(grp, xbc, conv0, lw["cw8"], lw["cb"], zsm[:, 2 * LANES:], lw["dtb"])
    if grp.per_token:
        conv_new = jnp.stack([conv0[:, 1], conv0[:, 2], xbc], axis=1)
        h0 = jnp.transpose(ssm0.reshape(nchain, HD, SSM_STATE), (1, 2, 0))
        rep = lambda m: jnp.transpose(jnp.repeat(m.reshape(B, SSM_GROUPS, SSM_STATE), HEADS // SSM_GROUPS, axis=1)
                                      .reshape(nchain, SSM_STATE))
        ysd, hn = ssd_step(h0, jnp.transpose(xs.reshape(nchain, HD)), rep(bc[:, :SSM_GROUPS * SSM_STATE]),
                           rep(bc[:, SSM_GROUPS * SSM_STATE:]), dtp[:, :HEADS].reshape(1, nchain),
                           jnp.tile(lw["alog_row"][:, :HEADS], (1, B)))
        ysd = jnp.transpose(ysd).reshape(T, SSM_DIM)
        ssm_new = jnp.transpose(hn, (2, 0, 1)).reshape(B, HEADS, HD, SSM_STATE)
    else:
        conv_new = xbc.reshape(B, L, CONV_DIM)[:, L - (CONV_W - 1):]
        dtpT = jnp.transpose(dtp.reshape(B, L, LANES), (0, 2, 1))
        ysd, sfin = ssd_chunk(B, L, xs, bc, dtp, dtpT, lw["alog_col"], lw["alog_row"], consts["expand"])
        ssm_new = sfin.reshape(B, HEADS, HD, SSM_STATE)
    yb = ssd_post(grp, ysd, xs, zg, lw["dfull"], lw["nw"])

    q, k, vh, lat, kpe = mla_prep(grp, zq, zsm[:, :2 * LANES], consts["cs"], lw["qn"], lw["kvn"],
                                  lw["wq2"], lw["wkv_prep"], lw["gq"], lw["gk"])
    yc = attend(l, q, k, vh, lat, kpe, lw)
    lat_out = lat.reshape(B, L, KV_LORA)
    kpe_out = kpe[:, NOPE_DIM:QK_DIM].reshape(B, L, ROPE_DIM)

    merged = merge_mm(l, ya, yb, yc, W["w_branch"], gates, tm, D // 4)
    x = mm(merged, W["w_out"], tm, D // 2, layer=l, epi="residual",
           extra=(x, g1), extra_specs=(pl.BlockSpec((tm, D // 2), lambda j, i: (i, j)),
                                       grp.vec_spec(D // 2, "ji")))
    states = (lat_out, kpe_out, shift_new, wkv_new, conv_new, ssm_new)
    return x, states, v_first


def _moe(l, h2, logits, lw, W, tm_r, tm_e, tie=None):
    T, D = h2.shape
    N = T * TOP_K
    gate, idx = route(logits, lw["router_bias"], tm_r)
    eidx = idx[:, :TOP_K]
    wsel = jnp.take_along_axis(gate, eidx, axis=1)
    i32 = jnp.int32
    experts = jnp.arange(N_EXPERTS, dtype=i32)
    flat_e = eidx.reshape(-1)
    order = jnp.argsort(flat_e, stable=True).astype(i32)
    inv = jnp.argsort(order).astype(i32)
    onehot = flat_e[:, None] == experts[None, :]
    counts = jnp.sum(onehot, axis=0, dtype=i32)
    starts = jnp.cumsum(counts) - counts
    pcounts = ((counts + tm_e - 1) // tm_e) * tm_e
    pends = jnp.cumsum(pcounts)
    pstarts = pends - pcounts
    lookup = lambda oh, table: jnp.sum(jnp.where(oh, table[None, :], 0), axis=1, dtype=i32)
    M = -(-N // tm_e) * tm_e + N_EXPERTS * tm_e
    slot = jnp.arange(M, dtype=i32)
    e_slot = jnp.minimum(jnp.sum(pends[None, :] <= slot[:, None], axis=1, dtype=i32), N_EXPERTS - 1)
    oh_slot = e_slot[:, None] == experts[None, :]
    off = slot - lookup(oh_slot, pstarts)
    valid = off < lookup(oh_slot, counts)
    rank = jnp.clip(lookup(oh_slot, starts) + off, 0, N - 1)
    src_flat = jnp.take(order, rank, mode="clip")
    src_tok = jnp.where(valid, src_flat // TOP_K, slot % T)
    row_w = jnp.where(valid, jnp.take(wsel.reshape(-1), src_flat, mode="clip"), 0.0)
    dest = lookup(onehot, pstarts) + inv - lookup(onehot, starts)
    tile_expert = e_slot[::tm_e]
    n_used = (pends[-1] // tm_e).reshape(1)
    xs = jnp.take(h2, src_tok, axis=0, mode="clip")
    if tie is not None:
        xs, tie = lax.optimization_barrier((xs, tie))
    ys = expert_ffn(l, tile_expert, n_used, xs, W["w_exp_gate"], W["w_exp_up"], W["w_exp_down"],
                    row_w.reshape(M, 1), tm_e)
    dest_k = jnp.transpose(dest.reshape(T, TOP_K)).reshape(-1)
    return jnp.take(ys, dest_k, axis=0, mode="clip").reshape(TOP_K, T, D), tie


def kernel(x_prompt, x_sample, cache_mla_latent, cache_mla_rope, page_table, state_rwkv_shift, state_rwkv_wkv, state_ssm_conv, state_ssm, c_prompt, c_sample, norm_mix, norm_ffn, w_ada, b_ada, w_in, rwkv_mu, rwkv_vecs, rwkv_w_up, rwkv_a_up, rwkv_g_up, rwkv_v0, rwkv_v_down, rwkv_v_up, ssm_conv_w, ssm_conv_b, ssm_dt_bias, ssm_a_log, ssm_d, ssm_norm_w, mla_q_norm, mla_q_up, mla_kv_norm, mla_kv_up, mla_qk_gain_q, mla_qk_gain_k, w_branch, w_out, w_router, router_bias, w_exp_gate, w_exp_up, w_exp_down, w_sh_gate, w_sh_up, w_sh_down):
    W = dict(norm_mix=norm_mix, norm_ffn=norm_ffn, w_ada=w_ada, b_ada=b_ada, w_in=w_in,
             rwkv_mu=rwkv_mu, rwkv_vecs=rwkv_vecs, rwkv_w_up=rwkv_w_up, rwkv_a_up=rwkv_a_up,
             rwkv_g_up=rwkv_g_up, rwkv_v0=rwkv_v0, rwkv_v_down=rwkv_v_down, rwkv_v_up=rwkv_v_up,
             ssm_conv_w=ssm_conv_w, ssm_conv_b=ssm_conv_b, ssm_dt_bias=ssm_dt_bias,
             ssm_a_log=ssm_a_log, ssm_d=ssm_d, ssm_norm_w=ssm_norm_w, mla_q_norm=mla_q_norm,
             mla_q_up=mla_q_up, mla_kv_norm=mla_kv_norm, mla_kv_up=mla_kv_up,
             mla_qk_gain_q=mla_qk_gain_q, mla_qk_gain_k=mla_qk_gain_k, w_branch=w_branch,
             w_out=w_out, w_router=w_router, router_bias=router_bias, w_exp_gate=w_exp_gate,
             w_exp_up=w_exp_up, w_exp_down=w_exp_down, w_sh_gate=w_sh_gate, w_sh_up=w_sh_up,
             w_sh_down=w_sh_down)
    depth = w_in.shape[0]
    D = D_MODEL
    Bp, Lp = x_prompt.shape[0], x_prompt.shape[1]
    Bs, Ls = x_sample.shape[0], x_sample.shape[1]
    assert Ls == 1
    n_pages = page_table.shape[1]
    past_len = n_pages * PAGE
    gp = Group(Bp, Lp, min(TM_PROMPT, Lp))
    gs = Group(Bs, 1, Bs)
    tq = min(TQ_FLASH, Lp)

    ex = np.zeros((LANES, SSM_DIM), np.float32)
    for hh in range(HEADS):
        ex[hh, hh * HD:(hh + 1) * HD] = 1.0
    pm = np.zeros((LANES, ROPE_DIM), np.float32)
    for j in range(ROPE_DIM):
        pm[NOPE_DIM + j, j] = 1.0
    consts_p = dict(bd64=_block_diag01(MXU_DIM, HD), expand=jnp.asarray(ex, BF16),
                    cs=_rope_tables(jnp.arange(Lp, dtype=jnp.int32)))
    consts_s = dict(bd64=consts_p["bd64"], expand=consts_p["expand"],
                    cs=_rope_tables(past_len + jnp.arange(1, dtype=jnp.int32)))
    pm_b = jnp.asarray(pm, BF16)
    G = DECODE_PAGES if n_pages % DECODE_PAGES == 0 else 2
    cache_kpe_t = jnp.swapaxes(cache_mla_rope, 2, 3)

    def attend_prompt(l, q, k, vh, lat, kpe, lw):
        return flash_causal(Bp, Lp, q, k, vh, tq)

    def attend_paged(l, q, k, vh, lat, kpe, lw):
        ql, qr = mla_absorb(q, lw["wkv2d"], lw["gk"], pm_b)
        ql = jnp.transpose(ql, (1, 0, 2))
        qr = jnp.transpose(qr, (1, 0, 2))
        nlat = jnp.pad(lat[:, None, :], ((0, 0), (0, PAGE - 1), (0, 0)))
        nkpe_t = jnp.pad(kpe[:, NOPE_DIM:QK_DIM, None], ((0, 0), (0, 0), (0, PAGE - 1)))
        acc = mla_decode(l, cache_mla_latent, cache_kpe_t, page_table, ql, qr, lw["wk_t"], nlat, nkpe_t, G)
        return mla_vup(jnp.transpose(acc, (1, 0, 2)), lw["wkv2d"])

    xp = x_prompt.reshape(Bp * Lp, D)
    xsm = x_sample.reshape(Bs, D)
    cp8 = jnp.pad(c_prompt, ((0, SUBLANES - Bp % SUBLANES if Bp % SUBLANES else 0), (0, 0)))
    outs_p = [[] for _ in range(6)]
    outs_s = [[] for _ in range(6)]
    vf_p = vf_s = None
    assert tq == gp.tm
    for l in range(depth):
        lw = _layer_weights(l, W)
        bada = b_ada[l].reshape(1, 6 * D)
        bspec = pl.BlockSpec((1, D // 2), lambda j, i: (0, j))
        mod_p = mm(cp8, w_ada, cp8.shape[0], D // 2, layer=l, act_in="silu", epi="bias",
                   extra=(bada,), extra_specs=(bspec,))[:Bp]
        mod_s = mm(c_sample, w_ada, Bs, D // 2, layer=l, act_in="silu", epi="bias",
                   extra=(bada,), extra_specs=(bspec,))
        mods_p = [gp.vec(mod_p[:, i * D:(i + 1) * D]) for i in range(6)]
        mods_s = [gs.vec(mod_s[:, i * D:(i + 1) * D]) for i in range(6)]
        st_s = (state_rwkv_shift[l], state_rwkv_wkv[l], state_ssm_conv[l], state_ssm[l])
        xp, st_p_new, vf_p = _mix_block(gp, l, xp, mods_p, (None, None, None, None), lw, W, consts_p, vf_p,
                                        attend_prompt)
        xsm, st_s_new, vf_s = _mix_block(gs, l, xsm, mods_s, st_s, lw, W, consts_s, vf_s, attend_paged)
        for lst, arr in zip(outs_p, st_p_new):
            lst.append(arr)
        for lst, arr in zip(outs_s, st_s_new):
            lst.append(arr)
        half = D // 2

        def ffn(grp, x, mods, tm_r, tm_e, tie=None):
            h2, logits = norm_mod(grp, x, norm_ffn[l], mods[4], mods[3], lw["w_router"])
            y8, tie = _moe(l, h2, logits, lw, W, tm_r, tm_e, tie)
            hid = glu_mm(l, h2, w_sh_gate, w_sh_up, tm_r)
            out = mm(hid, w_sh_down, grp.tm, half, layer=l, epi="residual_moe",
                     extra=(x, mods[5], y8),
                     extra_specs=(pl.BlockSpec((grp.tm, half), lambda j, i: (i, j)), grp.vec_spec(half, "ji"),
                                  pl.BlockSpec((TOP_K, grp.tm, half), lambda j, i: (0, i, j))))
            return out, tie

        xp, xsm = ffn(gp, xp, mods_p, min(TM_MOE, gp.T), TM_EXPERT, tie=xsm)
        xsm, _ = ffn(gs, xsm, mods_s, min(TM_MOE, gs.T), min(TM_EXPERT_SMALL, TM_EXPERT))
        xp, xsm = lax.optimization_barrier((xp, xsm))
    y_prompt = xp.reshape(Bp, Lp, D)
    y_sample = xsm.reshape(Bs, 1, D)
    sp = [jnp.stack(o) for o in outs_p]
    ss = [jnp.stack(o) for o in outs_s]
    ss[0] = ss[0].reshape(depth, Bs, 1, KV_LORA)
    ss[1] = ss[1].reshape(depth, Bs, 1, ROPE_DIM)
    return (y_prompt, y_sample, sp[0], sp[1], sp[2], sp[3], sp[4], sp[5],
            ss[0], ss[1], ss[2], ss[3], ss[4], ss[5])
```
